```python
import jax, jax.numpy as jnp
from jax import lax
import numpy as np

D_MODEL = 2048
BATCH = 4
SEQ = 4096
DEPTH = 1

A_HEADS = 8
A_HEAD_DIM = 128
A_WIDTH = A_HEADS * A_HEAD_DIM
CONV_WIDTH = 4
CHUNK_A = 64
B_GROUPS = 8
B_GROUP_DIM = 128
B_WIDTH = B_GROUPS * B_GROUP_DIM
CHUNK_B = 128
MIX_WIDTH = A_WIDTH + B_WIDTH
IN_SIZES = (3 * A_WIDTH, A_WIDTH, A_HEADS, A_HEADS, B_WIDTH, B_WIDTH, B_WIDTH)
IN_WIDTH = int(sum(IN_SIZES))
IN_SPLITS = tuple(int(s) for s in np.cumsum(IN_SIZES)[:-1])
EPS = 1e-6

kernel_name = "hymba_style_gdn_gmlp_hybrid"


def rms_norm(x, w):
    xf = x.astype(jnp.float32)
    y = xf * lax.rsqrt(jnp.mean(xf * xf, axis=-1, keepdims=True) + EPS)
    return (y * w.astype(jnp.float32)).astype(x.dtype)


def layer_norm(x, w, b):
    xf = x.astype(jnp.float32)
    mu = jnp.mean(xf, axis=-1, keepdims=True)
    var = jnp.mean(jnp.square(xf - mu), axis=-1, keepdims=True)
    y = (xf - mu) * lax.rsqrt(var + EPS)
    return (y * w.astype(jnp.float32) + b.astype(jnp.float32)).astype(x.dtype)


def l2_normalize(x):
    return x * lax.rsqrt(jnp.sum(x * x, axis=-1, keepdims=True) + EPS)


def causal_depthwise_conv(x, w):
    C = x.shape[-1]
    return lax.conv_general_dilated(
        x, w[:, None, :].astype(x.dtype), window_strides=(1,),
        padding=[(CONV_WIDTH - 1, 0)],
        dimension_numbers=("NWC", "WIO", "NWC"), feature_group_count=C)


def chunk_gated_delta_rule(q, k, v, g, beta):
    B, T, H, D = q.shape
    N, C = T // CHUNK_A, CHUNK_A
    chunk4 = lambda t: t.reshape(B, N, C, H, D).transpose(0, 3, 1, 2, 4)
    chunk3 = lambda t: t.reshape(B, N, C, H).transpose(0, 3, 1, 2)
    q = chunk4(q) * (D ** -0.5)
    k, v = chunk4(k), chunk4(v)
    beta, g = chunk3(beta), chunk3(g)
    kb = k * beta[..., None]
    vb = v * beta[..., None]
    gc = jnp.cumsum(g, axis=-1)
    incl = jnp.tril(jnp.ones((C, C), dtype=bool))
    strict = jnp.tril(jnp.ones((C, C), dtype=bool), k=-1)
    diff = gc[..., :, None] - gc[..., None, :]
    decay = jnp.where(incl, jnp.exp(jnp.where(incl, diff, 0.0)), 0.0)
    L = jnp.where(strict, jnp.einsum('bhnid,bhnjd->bhnij', kb, k) * decay, 0.0)
    rhs = jnp.concatenate([vb, kb * jnp.exp(gc)[..., None]], axis=-1)
    sol = lax.linalg.triangular_solve(L + jnp.eye(C, dtype=L.dtype), rhs,
                                      left_side=True, lower=True, unit_diagonal=True)
    u, w = sol[..., :D], sol[..., D:]
    attn = jnp.where(incl, jnp.einsum('bhnid,bhnjd->bhnij', q, k) * decay, 0.0)
    xs = tuple(jnp.moveaxis(t, 2, 0) for t in (q, k, u, w, attn, gc))

    def step(S, inp):
        qi, ki, ui, wi, ai, gi = inp
        v_new = ui - jnp.einsum('bhcd,bhde->bhce', wi, S)
        o = (jnp.einsum('bhcd,bhde->bhce', qi * jnp.exp(gi)[..., None], S)
             + jnp.einsum('bhij,bhje->bhie', ai, v_new))
        g_last = gi[..., -1]
        k_dec = ki * jnp.exp(g_last[..., None] - gi)[..., None]
        S = S * jnp.exp(g_last)[..., None, None] + jnp.einsum('bhcd,bhce->bhde', k_dec, v_new)
        return S, o

    S0 = jnp.zeros((B, H, D, D), dtype=jnp.float32)
    _, o = lax.scan(step, S0, xs)
    return o.transpose(1, 0, 3, 2, 4).reshape(B, T, H, D)


def chunked_causal_sgu(v, w_s, b_s):
    Bn, T, _ = v.shape
    N = T // CHUNK_B
    vc = v.reshape(Bn, N, CHUNK_B, B_GROUPS, B_GROUP_DIM)
    mask = jnp.tril(jnp.ones((CHUNK_B, CHUNK_B), dtype=bool))
    w_m = jnp.where(mask[None], w_s, 0.0).astype(v.dtype)
    s = jnp.einsum('gts,bnsgc->bntgc', w_m, vc) + b_s.T.astype(v.dtype)[None, None, :, :, None]
    return s.reshape(Bn, T, B_WIDTH)


def setup_inputs(seed: int = 0) -> dict:
    key = jax.random.key(seed)
    ks = jax.random.split(key, 14)
    f32 = jnp.float32
    x = jax.random.normal(ks[0], (BATCH, SEQ, D_MODEL), f32)
    norm_w = 1.0 + 0.02 * jax.random.normal(ks[1], (DEPTH, D_MODEL), f32)
    w_in = jax.random.normal(ks[2], (DEPTH, D_MODEL, IN_WIDTH), f32) * D_MODEL ** -0.5
    conv_w = jax.random.normal(ks[3], (DEPTH, CONV_WIDTH, 3 * A_WIDTH), f32) * CONV_WIDTH ** -0.5
    a_log = jnp.log(jax.random.uniform(ks[4], (DEPTH, A_HEADS), f32, 1.0, 16.0))
    dt = jnp.exp(jax.random.uniform(ks[5], (DEPTH, A_HEADS), f32,
                                    np.log(1e-3).astype(np.float32), np.log(1e-1).astype(np.float32)))
    dt_bias = dt + jnp.log(-jnp.expm1(-dt))
    head_norm_w = 1.0 + 0.02 * jax.random.normal(ks[6], (DEPTH, A_HEAD_DIM), f32)
    sgu_ln_w = 1.0 + 0.02 * jax.random.normal(ks[7], (DEPTH, B_WIDTH), f32)
    sgu_ln_b = 0.02 * jax.random.normal(ks[8], (DEPTH, B_WIDTH), f32)
    w_spatial = jax.random.normal(ks[9], (DEPTH, B_GROUPS, CHUNK_B, CHUNK_B), f32) * CHUNK_B ** -0.5
    b_spatial = 1.0 + 0.02 * jax.random.normal(ks[10], (DEPTH, B_GROUPS, CHUNK_B), f32)
    w_out = jax.random.normal(ks[11], (DEPTH, MIX_WIDTH, D_MODEL), f32) * MIX_WIDTH ** -0.5
    final_norm_w = 1.0 + 0.02 * jax.random.normal(ks[12], (D_MODEL,), f32)
    return {"x": x, "norm_w": norm_w, "w_in": w_in, "conv_w": conv_w, "a_log": a_log,
            "dt_bias": dt_bias, "head_norm_w": head_norm_w, "sgu_ln_w": sgu_ln_w,
            "sgu_ln_b": sgu_ln_b, "w_spatial": w_spatial, "b_spatial": b_spatial,
            "w_out": w_out, "final_norm_w": final_norm_w}


def reference(x, norm_w, w_in, conv_w, a_log, dt_bias, head_norm_w, sgu_ln_w,
              sgu_ln_b, w_spatial, b_spatial, w_out, final_norm_w):
    Bn, T, _ = x.shape
    h = x
    for l in range(DEPTH):
        xn = rms_norm(h, norm_w[l])
        proj = xn @ w_in[l].astype(x.dtype)
        qkv, z_a, b_raw, a_raw, u_b, v_b, z_b = jnp.split(proj, IN_SPLITS, axis=-1)

        qkv = jax.nn.silu(causal_depthwise_conv(qkv, conv_w[l])).astype(jnp.float32)
        q, k, v = jnp.split(qkv, 3, axis=-1)
        q = l2_normalize(q.reshape(Bn, T, A_HEADS, A_HEAD_DIM))
        k = l2_normalize(k.reshape(Bn, T, A_HEADS, A_HEAD_DIM))
        v = v.reshape(Bn, T, A_HEADS, A_HEAD_DIM)
        beta = jax.nn.sigmoid(b_raw.astype(jnp.float32))
        g = -jnp.exp(a_log[l].astype(jnp.float32)) * jax.nn.softplus(
            a_raw.astype(jnp.float32) + dt_bias[l].astype(jnp.float32))
        o_a = chunk_gated_delta_rule(q, k, v, g, beta)
        o_a = rms_norm(o_a, head_norm_w[l]).astype(x.dtype)
        o_a = (o_a * jax.nn.silu(z_a.reshape(Bn, T, A_HEADS, A_HEAD_DIM))).reshape(Bn, T, A_WIDTH)

        v_n = layer_norm(v_b, sgu_ln_w[l], sgu_ln_b[l])
        o_b = u_b * chunked_causal_sgu(v_n, w_spatial[l], b_spatial[l]) * jax.nn.silu(z_b)

        mix = jnp.concatenate([o_a, o_b], axis=-1) @ w_out[l].astype(x.dtype)
        h = h + mix
    return rms_norm(h, final_norm_w)
```

```python
import functools

import jax
import jax.numpy as jnp
from jax import lax
from jax.experimental import pallas as pl
from jax.experimental.pallas import tpu as pltpu

F32 = jnp.float32
BF16 = jnp.bfloat16

EPS = 1e-6
A_HEADS = 8
HEAD_DIM = 128
A_WIDTH = A_HEADS * HEAD_DIM
CONV_WIDTH = 4
CHUNK_A = 64
B_GROUPS = 8
GROUP_DIM = 128
B_WIDTH = B_GROUPS * GROUP_DIM
CHUNK_B = 128
LANES = 128
CONV_HALO = 8

COL_ZA, COL_UB, COL_VB, COL_ZB = 3, 4, 5, 6
MAIN_WIDTH = 7 * 1024

PROJ_TM, PROJ_TN = 1024, 1024
GDN_TT = 128
SGU_TT = 256
OUT_TM = 512
VMEM_LIMIT = 48 * 1024 * 1024


def _mm(a, b):
    return jnp.dot(a.astype(BF16), b.astype(BF16), preferred_element_type=F32)


def _mm_nt(a, b):
    return lax.dot_general(a.astype(BF16), b.astype(BF16), (((1,), (1,)), ((), ())),
                           preferred_element_type=F32)


def _mm_tn(a, b):
    return lax.dot_general(a.astype(BF16), b.astype(BF16), (((0,), (0,)), ((), ())),
                           preferred_element_type=F32)


def _mm_split(a01, b):
    hi = b.astype(BF16)
    lo = (b - hi.astype(F32)).astype(BF16)
    return (jnp.dot(a01, hi, preferred_element_type=F32)
            + jnp.dot(a01, lo, preferred_element_type=F32))


def _silu(y):
    return y * (1.0 / (1.0 + jnp.exp(-y)))


def _in_proj_kernel(x_ref, nw_ref, w_ref, wg_ref, out_ref, gate_ref, xn_ref):
    @pl.when(pl.program_id(1) == 0)
    def _():
        x = x_ref[...]
        ms = jnp.mean(x * x, axis=-1, keepdims=True)
        xn = ((x * lax.rsqrt(ms + EPS)) * nw_ref[...]).astype(BF16)
        xn_ref[...] = xn
        gate_ref[...] = jnp.dot(xn, wg_ref[...], preferred_element_type=F32)

    out_ref[...] = jnp.dot(xn_ref[...], w_ref[...], preferred_element_type=F32).astype(BF16)


def _in_proj(x2, norm_w, w_main, w_gate):
    m, d = x2.shape
    return pl.pallas_call(
        _in_proj_kernel,
        grid=(m // PROJ_TM, MAIN_WIDTH // PROJ_TN),
        in_specs=[
            pl.BlockSpec((PROJ_TM, d), lambda i, j: (i, 0)),
            pl.BlockSpec((1, d), lambda i, j: (0, 0)),
            pl.BlockSpec((d, PROJ_TN), lambda i, j: (0, j)),
            pl.BlockSpec((d, 2 * LANES), lambda i, j: (0, 0)),
        ],
        out_specs=[
            pl.BlockSpec((PROJ_TM, PROJ_TN), lambda i, j: (i, j)),
            pl.BlockSpec((PROJ_TM, 2 * LANES), lambda i, j: (i, 0)),
        ],
        out_shape=[
            jax.ShapeDtypeStruct((m, MAIN_WIDTH), BF16),
            jax.ShapeDtypeStruct((m, 2 * LANES), F32),
        ],
        scratch_shapes=[pltpu.VMEM((PROJ_TM, d), BF16)],
        compiler_params=pltpu.CompilerParams(
            dimension_semantics=("parallel", "arbitrary"), vmem_limit_bytes=VMEM_LIMIT),
        name="in_proj",
    )(x2, norm_w, w_main, w_gate)


def _gdn_kernel(qkv_ref, za_ref, gate_ref, convw_ref, alog_ref, dtb_ref, hnw_ref,
                oa_ref, xpad_ref, act_ref, s_ref):
    tt = GDN_TT
    c = CHUNK_A
    t = pl.program_id(1)

    @pl.when(t == 0)
    def _():
        xpad_ref[0:CONV_HALO, :] = jnp.zeros((CONV_HALO, 3 * A_WIDTH), F32)
        s_ref[...] = jnp.zeros_like(s_ref)

    @pl.when(t > 0)
    def _():
        xpad_ref[0:CONV_HALO, :] = xpad_ref[tt:tt + CONV_HALO, :]

    xpad_ref[CONV_HALO:CONV_HALO + tt, :] = qkv_ref[...].astype(F32)

    for cb in range(3 * A_HEADS):
        cols = slice(cb * LANES, (cb + 1) * LANES)
        acc = None
        for k in range(CONV_WIDTH):
            r0 = CONV_HALO - (CONV_WIDTH - 1) + k
            term = convw_ref[k:k + 1, cols] * xpad_ref[r0:r0 + tt, cols]
            acc = term if acc is None else acc + term
        y = _silu(acc)
        if cb < 2 * A_HEADS:
            y = y * lax.rsqrt(jnp.sum(y * y, axis=-1, keepdims=True) + EPS)
        if cb < A_HEADS:
            y = y * (HEAD_DIM ** -0.5)
        act_ref[:, cols] = y

    gates = gate_ref[...]
    beta = 1.0 / (1.0 + jnp.exp(-gates[:, :LANES]))
    a_in = gates[:, LANES:] + dtb_ref[...]
    softplus = jnp.maximum(a_in, 0.0) + jnp.log(1.0 + jnp.exp(-jnp.abs(a_in)))
    g = -jnp.exp(alog_ref[...]) * softplus

    ri = lax.broadcasted_iota(jnp.int32, (tt, tt), 0)
    ci = lax.broadcasted_iota(jnp.int32, (tt, tt), 1)
    same_chunk = (ri // c) == (ci // c)
    tri = jnp.where(same_chunk & (ci <= ri), 1.0, 0.0).astype(BF16)
    ones_bd = jnp.where(same_chunk, 1.0, 0.0).astype(BF16)
    gc = _mm_split(tri, g)
    gl = _mm_split(ones_bd, g)
    e_gc = jnp.exp(gc)
    e_kd = jnp.exp(gl - gc)
    e_gl = jnp.exp(gl)

    row = lax.broadcasted_iota(jnp.int32, (c, c), 0)
    col = lax.broadcasted_iota(jnp.int32, (c, c), 1)
    incl = row >= col
    strict = row > col
    eye = jnp.where(row == col, 1.0, 0.0).astype(F32)
    hnw = hnw_ref[...]

    for n in range(tt // c):
        rows = slice(n * c, (n + 1) * c)
        gc_n = gc[rows, :]
        gc_t = gc_n.T
        for h in range(A_HEADS):
            q = act_ref[rows, h * LANES:(h + 1) * LANES]
            k = act_ref[rows, A_WIDTH + h * LANES:A_WIDTH + (h + 1) * LANES]
            v = act_ref[rows, 2 * A_WIDTH + h * LANES:2 * A_WIDTH + (h + 1) * LANES]
            b_col = beta[rows, h:h + 1]
            gcol = gc_n[:, h:h + 1]
            grow = gc_t[h:h + 1, :]
            egc_col = e_gc[rows, h:h + 1]
            ekd_col = e_kd[rows, h:h + 1]
            egl_11 = e_gl[n * c:n * c + 1, h:h + 1]

            kb = k * b_col
            vb = v * b_col
            decay = jnp.where(incl, jnp.exp(jnp.where(incl, gcol - grow, 0.0)), 0.0)
            kk = _mm_nt(jnp.concatenate([q, kb], axis=0), k)
            attn = kk[:c] * decay
            lmat = jnp.where(strict, kk[c:] * decay, 0.0)
            p = eye - lmat
            m = _mm(lmat, lmat)
            for it in range(5):
                p = p + _mm(p, m)
                if it < 4:
                    m = _mm(m, m)
            sol = _mm(p, jnp.concatenate([vb, kb * egc_col], axis=1))
            u = sol[:, :HEAD_DIM]
            w = sol[:, HEAD_DIM:]
            s = s_ref[h]
            ws = _mm(jnp.concatenate([w, q * egc_col], axis=0), s)
            v_new = u - ws[:c]
            o = ws[c:] + _mm(attn, v_new)
            s_ref[h] = s * egl_11 + _mm_tn(k * ekd_col, v_new)

            on = (o * lax.rsqrt(jnp.mean(o * o, axis=-1, keepdims=True) + EPS)) * hnw
            za = za_ref[rows, h * LANES:(h + 1) * LANES].astype(F32)
            oa_ref[rows, h * LANES:(h + 1) * LANES] = (on * _silu(za)).astype(BF16)


def _gdn(p_main, gates, conv_w, alog_lane, dtb_lane, hnw, batch, seq):
    steps = seq // GDN_TT
    return pl.pallas_call(
        _gdn_kernel,
        grid=(batch, steps),
        in_specs=[
            pl.BlockSpec((GDN_TT, 3 * A_WIDTH), lambda b, t: (b * steps + t, 0)),
            pl.BlockSpec((GDN_TT, A_WIDTH), lambda b, t: (b * steps + t, COL_ZA)),
            pl.BlockSpec((GDN_TT, 2 * LANES), lambda b, t: (b * steps + t, 0)),
            pl.BlockSpec((CONV_WIDTH, 3 * A_WIDTH), lambda b, t: (0, 0)),
            pl.BlockSpec((1, LANES), lambda b, t: (0, 0)),
            pl.BlockSpec((1, LANES), lambda b, t: (0, 0)),
            pl.BlockSpec((1, HEAD_DIM), lambda b, t: (0, 0)),
        ],
        out_specs=pl.BlockSpec((GDN_TT, A_WIDTH), lambda b, t: (b * steps + t, 0)),
        out_shape=jax.ShapeDtypeStruct((batch * seq, A_WIDTH), BF16),
        scratch_shapes=[
            pltpu.VMEM((CONV_HALO + GDN_TT, 3 * A_WIDTH), F32),
            pltpu.VMEM((GDN_TT, 3 * A_WIDTH), F32),
            pltpu.VMEM((A_HEADS, HEAD_DIM, HEAD_DIM), F32),
        ],
        compiler_params=pltpu.CompilerParams(
            dimension_semantics=("parallel", "arbitrary"), vmem_limit_bytes=VMEM_LIMIT),
        name="gdn",
    )(p_main, p_main, gates, conv_w, alog_lane, dtb_lane, hnw)


def _sgu_kernel(ub_ref, vb_ref, zb_ref, lnw_ref, lnb_ref, ws_ref, bs_ref, ob_ref):
    v = vb_ref[...].astype(F32)
    mu = jnp.mean(v, axis=-1, keepdims=True)
    vc = v - mu
    var = jnp.mean(vc * vc, axis=-1, keepdims=True)
    vn = (vc * lax.rsqrt(var + EPS)) * lnw_ref[...] + lnb_ref[...]

    row = lax.broadcasted_iota(jnp.int32, (CHUNK_B, CHUNK_B), 0)
    col = lax.broadcasted_iota(jnp.int32, (CHUNK_B, CHUNK_B), 1)
    causal = row >= col
    bias = bs_ref[...]
    for gidx in range(B_GROUPS):
        cols = slice(gidx * GROUP_DIM, (gidx + 1) * GROUP_DIM)
        w_m = jnp.where(causal, ws_ref[gidx], 0.0)
        b_col = bias[:, gidx:gidx + 1]
        for n in range(SGU_TT // CHUNK_B):
            rows = slice(n * CHUNK_B, (n + 1) * CHUNK_B)
            s = _mm(w_m, vn[rows, cols]) + b_col
            u = ub_ref[rows, cols].astype(F32)
            z = zb_ref[rows, cols].astype(F32)
            ob_ref[rows, cols] = (u * s * _silu(z)).astype(BF16)


def _sgu(p_main, ln_w, ln_b, w_spatial, bias_t):
    m = p_main.shape[0]
    return pl.pallas_call(
        _sgu_kernel,
        grid=(m // SGU_TT,),
        in_specs=[
            pl.BlockSpec((SGU_TT, B_WIDTH), lambda i: (i, COL_UB)),
            pl.BlockSpec((SGU_TT, B_WIDTH), lambda i: (i, COL_VB)),
            pl.BlockSpec((SGU_TT, B_WIDTH), lambda i: (i, COL_ZB)),
            pl.BlockSpec((1, B_WIDTH), lambda i: (0, 0)),
            pl.BlockSpec((1, B_WIDTH), lambda i: (0, 0)),
            pl.BlockSpec((B_GROUPS, CHUNK_B, CHUNK_B), lambda i: (0, 0, 0)),
            pl.BlockSpec((CHUNK_B, LANES), lambda i: (0, 0)),
        ],
        out_specs=pl.BlockSpec((SGU_TT, B_WIDTH), lambda i: (i, 0)),
        out_shape=jax.ShapeDtypeStruct((m, B_WIDTH), BF16),
        compiler_params=pltpu.CompilerParams(
            dimension_semantics=("parallel",), vmem_limit_bytes=VMEM_LIMIT),
        name="sgu",
    )(p_main, p_main, p_main, ln_w, ln_b, w_spatial, bias_t)


def _out_proj_kernel(oa_ref, ob_ref, x_ref, wa_ref, wb_ref, fnw_ref, out_ref):
    mix = (jnp.dot(oa_ref[...], wa_ref[...], preferred_element_type=F32)
           + jnp.dot(ob_ref[...], wb_ref[...], preferred_element_type=F32))
    h = x_ref[...] + mix
    ms = jnp.mean(h * h, axis=-1, keepdims=True)
    out_ref[...] = (h * lax.rsqrt(ms + EPS)) * fnw_ref[...]


def _out_proj(o_a, o_b, x2, w_out_bf, final_norm_w):
    m, d = x2.shape
    return pl.pallas_call(
        _out_proj_kernel,
        grid=(m // OUT_TM,),
        in_specs=[
            pl.BlockSpec((OUT_TM, A_WIDTH), lambda i: (i, 0)),
            pl.BlockSpec((OUT_TM, B_WIDTH), lambda i: (i, 0)),
            pl.BlockSpec((OUT_TM, d), lambda i: (i, 0)),
            pl.BlockSpec((A_WIDTH, d), lambda i: (0, 0)),
            pl.BlockSpec((B_WIDTH, d), lambda i: (1, 0)),
            pl.BlockSpec((1, d), lambda i: (0, 0)),
        ],
        out_specs=pl.BlockSpec((OUT_TM, d), lambda i: (i, 0)),
        out_shape=jax.ShapeDtypeStruct((m, d), F32),
        compiler_params=pltpu.CompilerParams(
            dimension_semantics=("parallel",), vmem_limit_bytes=VMEM_LIMIT),
        name="out_proj",
    )(o_a, o_b, x2, w_out_bf, w_out_bf, final_norm_w)


def _lane_row(vec):
    return jnp.pad(vec.astype(F32), (0, LANES - vec.shape[0])).reshape(1, LANES)


def kernel(x, norm_w, w_in, conv_w, a_log, dt_bias, head_norm_w, sgu_ln_w, sgu_ln_b,
           w_spatial, b_spatial, w_out, final_norm_w):
    batch, seq, d = x.shape
    assert norm_w.shape[0] == 1, "single-layer problem"
    assert seq % GDN_TT == 0 and seq % SGU_TT == 0 and (batch * seq) % PROJ_TM == 0
    x2 = x.reshape(batch * seq, d)

    w = w_in[0]
    gate0 = 4 * A_WIDTH
    w_main = jnp.concatenate([w[:, :gate0], w[:, gate0 + 2 * A_HEADS:]], axis=1).astype(BF16)
    w_b = jnp.pad(w[:, gate0:gate0 + A_HEADS], ((0, 0), (0, LANES - A_HEADS)))
    w_a = jnp.pad(w[:, gate0 + A_HEADS:gate0 + 2 * A_HEADS], ((0, 0), (0, LANES - A_HEADS)))
    w_gate = jnp.concatenate([w_b, w_a], axis=1).astype(BF16)

    p_main, gates = _in_proj(x2, norm_w, w_main, w_gate)
    o_a = _gdn(p_main, gates, conv_w[0], _lane_row(a_log[0]), _lane_row(dt_bias[0]),
               head_norm_w, batch, seq)
    bias_t = jnp.pad(b_spatial[0].T.astype(F32), ((0, 0), (0, LANES - B_GROUPS)))
    o_b = _sgu(p_main, sgu_ln_w, sgu_ln_b, w_spatial[0], bias_t)
    out = _out_proj(o_a, o_b, x2, w_out[0].astype(BF16), final_norm_w.reshape(1, d))
    return out.reshape(batch, seq, d)
```

```python
import functools

import jax
import jax.numpy as jnp
from jax import lax
from jax.experimental import pallas as pl
from jax.experimental.pallas import tpu as pltpu

F32 = jnp.float32
BF16 = jnp.bfloat16

EPS = 1e-6
A_HEADS = 8
HEAD_DIM = 128
A_WIDTH = A_HEADS * HEAD_DIM
CONV_WIDTH = 4
CHUNK_A = 64
B_GROUPS = 8
GROUP_DIM = 128
B_WIDTH = B_GROUPS * GROUP_DIM
CHUNK_B = 128
LANES = 128
CONV_HALO = 8

COL_ZA, COL_UB, COL_VB, COL_ZB = 3, 4, 5, 6
MAIN_WIDTH = 7 * 1024

PROJ_TM, PROJ_TN = 1024, 1024
GDN_TT = 128
SGU_TT = 256
OUT_TM = 512
VMEM_LIMIT = 48 * 1024 * 1024


def _mm(a, b):
    return jnp.dot(a.astype(BF16), b.astype(BF16), preferred_element_type=F32)


def _mm_nt(a, b):
    return lax.dot_general(a.astype(BF16), b.astype(BF16), (((1,), (1,)), ((), ())),
                           preferred_element_type=F32)


def _mm_tn(a, b):
    return lax.dot_general(a.astype(BF16), b.astype(BF16), (((0,), (0,)), ((), ())),
                           preferred_element_type=F32)


def _mm_split(a01, b):
    hi = b.astype(BF16)
    lo = (b - hi.astype(F32)).astype(BF16)
    return (jnp.dot(a01, hi, preferred_element_type=F32)
            + jnp.dot(a01, lo, preferred_element_type=F32))


def _silu(y):
    return y * (1.0 / (1.0 + jnp.exp(-y)))


def _in_proj_kernel(x_ref, nw_ref, w_ref, wg_ref, out_ref, gate_ref, xn_ref):
    @pl.when(pl.program_id(1) == 0)
    def _():
        x = x_ref[...]
        ms = jnp.mean(x * x, axis=-1, keepdims=True)
        xn = ((x * lax.rsqrt(ms + EPS)) * nw_ref[...]).astype(BF16)
        xn_ref[...] = xn
        gate_ref[...] = jnp.dot(xn, wg_ref[...], preferred_element_type=F32)

    out_ref[...] = jnp.dot(xn_ref[...], w_ref[...], preferred_element_type=F32).astype(BF16)


def _in_proj(x2, norm_w, w_main, w_gate):
    m, d = x2.shape
    return pl.pallas_call(
        _in_proj_kernel,
        grid=(m // PROJ_TM, MAIN_WIDTH // PROJ_TN),
        in_specs=[
            pl.BlockSpec((PROJ_TM, d), lambda i, j: (i, 0)),
            pl.BlockSpec((1, d), lambda i, j: (0, 0)),
            pl.BlockSpec((d, PROJ_TN), lambda i, j: (0, j)),
            pl.BlockSpec((d, 2 * LANES), lambda i, j: (0, 0)),
        ],
        out_specs=[
            pl.BlockSpec((PROJ_TM, PROJ_TN), lambda i, j: (i, j)),
            pl.BlockSpec((PROJ_TM, 2 * LANES), lambda i, j: (i, 0)),
        ],
        out_shape=[
            jax.ShapeDtypeStruct((m, MAIN_WIDTH), BF16),
            jax.ShapeDtypeStruct((m, 2 * LANES), F32),
        ],
        scratch_shapes=[pltpu.VMEM((PROJ_TM, d), BF16)],
        compiler_params=pltpu.CompilerParams(
            dimension_semantics=("parallel", "arbitrary"), vmem_limit_bytes=VMEM_LIMIT),
        name="in_proj",
    )(x2, norm_w, w_main, w_gate)


def _gdn_kernel(qkv_ref, za_ref, gate_ref, convw_ref, alog_ref, dtb_ref, hnw_ref,
                oa_ref, xpad_ref, act_ref, s_ref):
    tt = GDN_TT
    c = CHUNK_A
    t = pl.program_id(1)

    @pl.when(t == 0)
    def _():
        xpad_ref[0:CONV_HALO, :] = jnp.zeros((CONV_HALO, 3 * A_WIDTH), F32)
        s_ref[...] = jnp.zeros_like(s_ref)

    @pl.when(t > 0)
    def _():
        xpad_ref[0:CONV_HALO, :] = xpad_ref[tt:tt + CONV_HALO, :]

    xpad_ref[CONV_HALO:CONV_HALO + tt, :] = qkv_ref[...].astype(F32)

    for cb in range(3 * A_HEADS):
        cols = slice(cb * LANES, (cb + 1) * LANES)
        acc = None
        for k in range(CONV_WIDTH):
            r0 = CONV_HALO - (CONV_WIDTH - 1) + k
            term = convw_ref[k:k + 1, cols] * xpad_ref[r0:r0 + tt, cols]
            acc = term if acc is None else acc + term
        y = _silu(acc)
        if cb < 2 * A_HEADS:
            y = y * lax.rsqrt(jnp.sum(y * y, axis=-1, keepdims=True) + EPS)
        if cb < A_HEADS:
            y = y * (HEAD_DIM ** -0.5)
        act_ref[:, cols] = y

    gates = gate_ref[...]
    beta = 1.0 / (1.0 + jnp.exp(-gates[:, :LANES]))
    a_in = gates[:, LANES:] + dtb_ref[...]
    softplus = jnp.maximum(a_in, 0.0) + jnp.log(1.0 + jnp.exp(-jnp.abs(a_in)))
    g = -jnp.exp(alog_ref[...]) * softplus

    ri = lax.broadcasted_iota(jnp.int32, (tt, tt), 0)
    ci = lax.broadcasted_iota(jnp.int32, (tt, tt), 1)
    same_chunk = (ri // c) == (ci // c)
    tri = jnp.where(same_chunk & (ci <= ri), 1.0, 0.0).astype(BF16)
    ones_bd = jnp.where(same_chunk, 1.0, 0.0).astype(BF16)
    gc = _mm_split(tri, g)
    gl = _mm_split(ones_bd, g)
    e_gc = jnp.exp(gc)
    e_kd = jnp.exp(gl - gc)
    e_gl = jnp.exp(gl)

    row = lax.broadcasted_iota(jnp.int32, (c, c), 0)
    col = lax.broadcasted_iota(jnp.int32, (c, c), 1)
    incl = row >= col
    strict = row > col
    eye = jnp.where(row == col, 1.0, 0.0).astype(F32)
    hnw = hnw_ref[...]

    n_chunks = tt // c
    pairs = [(n, h) for n in range(n_chunks) for h in range(A_HEADS)]
    gc_t = [gc[n * c:(n + 1) * c, :].T for n in range(n_chunks)]

    def rows_of(n):
        return slice(n * c, (n + 1) * c)

    def act(n, h, part):
        return act_ref[rows_of(n), part * A_WIDTH + h * LANES:part * A_WIDTH + (h + 1) * LANES]

    def colv(arr, n, h):
        return arr[rows_of(n), h:h + 1]

    kb, attn, lmat = {}, {}, {}
    for (n, h) in pairs:
        q, k = act(n, h, 0), act(n, h, 1)
        kb[n, h] = k * colv(beta, n, h)
        diff = colv(gc, n, h) - gc_t[n][h:h + 1, :]
        decay = jnp.where(incl, jnp.exp(jnp.where(incl, diff, 0.0)), 0.0)
        kk = _mm_nt(jnp.concatenate([q, kb[n, h]], axis=0), k)
        attn[n, h] = kk[:c] * decay
        lmat[n, h] = jnp.where(strict, kk[c:] * decay, 0.0)
    p = {key: eye - lmat[key] for key in pairs}
    m = {key: _mm(lmat[key], lmat[key]) for key in pairs}
    for it in range(5):
        p = {key: p[key] + _mm(p[key], m[key]) for key in pairs}
        if it < 4:
            m = {key: _mm(m[key], m[key]) for key in pairs}
    u, w = {}, {}
    for (n, h) in pairs:
        vb = act(n, h, 2) * colv(beta, n, h)
        sol = _mm(p[n, h], jnp.concatenate([vb, kb[n, h] * colv(e_gc, n, h)], axis=1))
        u[n, h] = sol[:, :HEAD_DIM]
        w[n, h] = sol[:, HEAD_DIM:]

    s = [s_ref[h] for h in range(A_HEADS)]
    for n in range(n_chunks):
        ws = [_mm(jnp.concatenate([w[n, h], act(n, h, 0) * colv(e_gc, n, h)], axis=0), s[h])
              for h in range(A_HEADS)]
        v_new = [u[n, h] - ws[h][:c] for h in range(A_HEADS)]
        o = [ws[h][c:] + _mm(attn[n, h], v_new[h]) for h in range(A_HEADS)]
        s = [s[h] * e_gl[n * c:n * c + 1, h:h + 1]
             + _mm_tn(act(n, h, 1) * colv(e_kd, n, h), v_new[h]) for h in range(A_HEADS)]
        for h in range(A_HEADS):
            on = (o[h] * lax.rsqrt(jnp.mean(o[h] * o[h], axis=-1, keepdims=True) + EPS)) * hnw
            za = za_ref[rows_of(n), h * LANES:(h + 1) * LANES].astype(F32)
            oa_ref[rows_of(n), h * LANES:(h + 1) * LANES] = (on * _silu(za)).astype(BF16)
    for h in range(A_HEADS):
        s_ref[h] = s[h]


def _gdn(p_main, gates, conv_w, alog_lane, dtb_lane, hnw, batch, seq):
    steps = seq // GDN_TT
    return pl.pallas_call(
        _gdn_kernel,
        grid=(batch, steps),
        in_specs=[
            pl.BlockSpec((GDN_TT, 3 * A_WIDTH), lambda b, t: (b * steps + t, 0)),
            pl.BlockSpec((GDN_TT, A_WIDTH), lambda b, t: (b * steps + t, COL_ZA)),
            pl.BlockSpec((GDN_TT, 2 * LANES), lambda b, t: (b * steps + t, 0)),
            pl.BlockSpec((CONV_WIDTH, 3 * A_WIDTH), lambda b, t: (0, 0)),
            pl.BlockSpec((1, LANES), lambda b, t: (0, 0)),
            pl.BlockSpec((1, LANES), lambda b, t: (0, 0)),
            pl.BlockSpec((1, HEAD_DIM), lambda b, t: (0, 0)),
        ],
        out_specs=pl.BlockSpec((GDN_TT, A_WIDTH), lambda b, t: (b * steps + t, 0)),
        out_shape=jax.ShapeDtypeStruct((batch * seq, A_WIDTH), BF16),
        scratch_shapes=[
            pltpu.VMEM((CONV_HALO + GDN_TT, 3 * A_WIDTH), F32),
            pltpu.VMEM((GDN_TT, 3 * A_WIDTH), F32),
            pltpu.VMEM((A_HEADS, HEAD_DIM, HEAD_DIM), F32),
        ],
        compiler_params=pltpu.CompilerParams(
            dimension_semantics=("parallel", "arbitrary"), vmem_limit_bytes=VMEM_LIMIT),
        name="gdn",
    )(p_main, p_main, gates, conv_w, alog_lane, dtb_lane, hnw)


def _sgu_kernel(ub_ref, vb_ref, zb_ref, lnw_ref, lnb_ref, ws_ref, bs_ref, ob_ref):
    v = vb_ref[...].astype(F32)
    mu = jnp.mean(v, axis=-1, keepdims=True)
    vc = v - mu
    var = jnp.mean(vc * vc, axis=-1, keepdims=True)
    vn = (vc * lax.rsqrt(var + EPS)) * lnw_ref[...] + lnb_ref[...]

    row = lax.broadcasted_iota(jnp.int32, (CHUNK_B, CHUNK_B), 0)
    col = lax.broadcasted_iota(jnp.int32, (CHUNK_B, CHUNK_B), 1)
    causal = row >= col
    bias = bs_ref[...]
    for gidx in range(B_GROUPS):
        cols = slice(gidx * GROUP_DIM, (gidx + 1) * GROUP_DIM)
        w_m = jnp.where(causal, ws_ref[gidx], 0.0)
        b_col = bias[:, gidx:gidx + 1]
        for n in range(SGU_TT // CHUNK_B):
            rows = slice(n * CHUNK_B, (n + 1) * CHUNK_B)
            s = _mm(w_m, vn[rows, cols]) + b_col
            u = ub_ref[rows, cols].astype(F32)
            z = zb_ref[rows, cols].astype(F32)
            ob_ref[rows, cols] = (u * s * _silu(z)).astype(BF16)


def _sgu(p_main, ln_w, ln_b, w_spatial, bias_t):
    m = p_main.shape[0]
    return pl.pallas_call(
        _sgu_kernel,
        grid=(m // SGU_TT,),
        in_specs=[
            pl.BlockSpec((SGU_TT, B_WIDTH), lambda i: (i, COL_UB)),
            pl.BlockSpec((SGU_TT, B_WIDTH), lambda i: (i, COL_VB)),
            pl.BlockSpec((SGU_TT, B_WIDTH), lambda i: (i, COL_ZB)),
            pl.BlockSpec((1, B_WIDTH), lambda i: (0, 0)),
            pl.BlockSpec((1, B_WIDTH), lambda i: (0, 0)),
            pl.BlockSpec((B_GROUPS, CHUNK_B, CHUNK_B), lambda i: (0, 0, 0)),
            pl.BlockSpec((CHUNK_B, LANES), lambda i: (0, 0)),
        ],
        out_specs=pl.BlockSpec((SGU_TT, B_WIDTH), lambda i: (i, 0)),
        out_shape=jax.ShapeDtypeStruct((m, B_WIDTH), BF16),
        compiler_params=pltpu.CompilerParams(
            dimension_semantics=("parallel",), vmem_limit_bytes=VMEM_LIMIT),
        name="sgu",
    )(p_main, p_main, p_main, ln_w, ln_b, w_spatial, bias_t)


def _out_proj_kernel(oa_ref, ob_ref, x_ref, wa_ref, wb_ref, fnw_ref, out_ref):
    mix = (jnp.dot(oa_ref[...], wa_ref[...], preferred_element_type=F32)
           + jnp.dot(ob_ref[...], wb_ref[...], preferred_element_type=F32))
    h = x_ref[...] + mix
    ms = jnp.mean(h * h, axis=-1, keepdims=True)
    out_ref[...] = (h * lax.rsqrt(ms + EPS)) * fnw_ref[...]


def _out_proj(o_a, o_b, x2, w_out_bf, final_norm_w):
    m, d = x2.shape
    return pl.pallas_call(
        _out_proj_kernel,
        grid=(m // OUT_TM,),
        in_specs=[
            pl.BlockSpec((OUT_TM, A_WIDTH), lambda i: (i, 0)),
            pl.BlockSpec((OUT_TM, B_WIDTH), lambda i: (i, 0)),
            pl.BlockSpec((OUT_TM, d), lambda i: (i, 0)),
            pl.BlockSpec((A_WIDTH, d), lambda i: (0, 0)),
            pl.BlockSpec((B_WIDTH, d), lambda i: (1, 0)),
            pl.BlockSpec((1, d), lambda i: (0, 0)),
        ],
        out_specs=pl.BlockSpec((OUT_TM, d), lambda i: (i, 0)),
        out_shape=jax.ShapeDtypeStruct((m, d), F32),
        compiler_params=pltpu.CompilerParams(
            dimension_semantics=("parallel",), vmem_limit_bytes=VMEM_LIMIT),
        name="out_proj",
    )(o_a, o_b, x2, w_out_bf, w_out_bf, final_norm_w)


def _lane_row(vec):
    return jnp.pad(vec.astype(F32), (0, LANES - vec.shape[0])).reshape(1, LANES)


def kernel(x, norm_w, w_in, conv_w, a_log, dt_bias, head_norm_w, sgu_ln_w, sgu_ln_b,
           w_spatial, b_spatial, w_out, final_norm_w):
    batch, seq, d = x.shape
    assert norm_w.shape[0] == 1, "single-layer problem"
    assert seq % GDN_TT == 0 and seq % SGU_TT == 0 and (batch * seq) % PROJ_TM == 0
    x2 = x.reshape(batch * seq, d)

    w = w_in[0]
    gate0 = 4 * A_WIDTH
    w_main = jnp.concatenate([w[:, :gate0], w[:, gate0 + 2 * A_HEADS:]], axis=1).astype(BF16)
    w_b = jnp.pad(w[:, gate0:gate0 + A_HEADS], ((0, 0), (0, LANES - A_HEADS)))
    w_a = jnp.pad(w[:, gate0 + A_HEADS:gate0 + 2 * A_HEADS], ((0, 0), (0, LANES - A_HEADS)))
    w_gate = jnp.concatenate([w_b, w_a], axis=1).astype(BF16)

    p_main, gates = _in_proj(x2, norm_w, w_main, w_gate)
    o_a = _gdn(p_main, gates, conv_w[0], _lane_row(a_log[0]), _lane_row(dt_bias[0]),
               head_norm_w, batch, seq)
    bias_t = jnp.pad(b_spatial[0].T.astype(F32), ((0, 0), (0, LANES - B_GROUPS)))
    o_b = _sgu(p_main, sgu_ln_w, sgu_ln_b, w_spatial[0], bias_t)
    out = _out_proj(o_a, o_b, x2, w_out[0].astype(BF16), final_norm_w.reshape(1, d))
    return out.reshape(batch, seq, d)
```

```python
import functools

import jax
import jax.numpy as jnp
from jax import lax
from jax.experimental import pallas as pl
from jax.experimental.pallas import tpu as pltpu

F32 = jnp.float32
BF16 = jnp.bfloat16

EPS = 1e-6
A_HEADS = 8
HEAD_DIM = 128
A_WIDTH = A_HEADS * HEAD_DIM
CONV_WIDTH = 4
CHUNK_A = 64
B_GROUPS = 8
GROUP_DIM = 128
B_WIDTH = B_GROUPS * GROUP_DIM
CHUNK_B = 128
LANES = 128
MXU_WIDTH = 256
CONV_HALO = 8

COL_Q, COL_K, COL_V, COL_ZA, COL_UB, COL_VB, COL_ZB = range(7)
BLOCK_W = 1024
MAIN_WIDTH = 7 * BLOCK_W
SUB_W = MXU_WIDTH
SUB_M = 512
N_SUB = BLOCK_W // SUB_W

PROJ_TM = 1024
GDN_TT = 256
SGU_TT = 256
OUT_TM = 512
VMEM_LIMIT = 52 * 1024 * 1024


def _mm(a, b):
    return jnp.dot(a.astype(BF16), b.astype(BF16), preferred_element_type=F32)


def _mm_nt(a, b):
    return lax.dot_general(a.astype(BF16), b.astype(BF16), (((1,), (1,)), ((), ())),
                           preferred_element_type=F32)


def _mm_tn(a, b):
    return lax.dot_general(a.astype(BF16), b.astype(BF16), (((0,), (0,)), ((), ())),
                           preferred_element_type=F32)


def _mm_split(a01, b):
    hi = b.astype(BF16)
    lo = (b - hi.astype(F32)).astype(BF16)
    return (jnp.dot(a01, hi, preferred_element_type=F32)
            + jnp.dot(a01, lo, preferred_element_type=F32))


def _silu(y):
    return y * (1.0 / (1.0 + jnp.exp(-y)))


def _in_proj_kernel(tiles_per_seq, x_ref, nw_ref, w_ref, wg_ref, cw_ref, lnw_ref, lnb_ref,
                    out_ref, gate_ref, xn_ref, accp_ref, halo_ref, full_ref):
    i = pl.program_id(0)
    j = pl.program_id(1)

    @pl.when(j == 0)
    def _():
        x = x_ref[...]
        ms = jnp.mean(x * x, axis=-1, keepdims=True)
        xn = ((x * lax.rsqrt(ms + EPS)) * nw_ref[...]).astype(BF16)
        xn_ref[...] = xn
        gate_ref[...] = jnp.dot(xn, wg_ref[...], preferred_element_type=F32)

    @pl.when((j == 0) & (i == 0))
    def _():
        halo_ref[...] = jnp.zeros_like(halo_ref)

    n_halves = PROJ_TM // SUB_M
    units = [(c, r) for c in range(N_SUB) for r in range(n_halves)]

    def unit_dot(c, r):
        return jnp.dot(xn_ref[r * SUB_M:(r + 1) * SUB_M, :], w_ref[:, c * SUB_W:(c + 1) * SUB_W],
                       preferred_element_type=F32)

    def pipelined(stage, finish):
        for n, unit in enumerate(units):
            stage(*unit)
            if n > 0:
                finish(*units[n - 1])
        finish(*units[-1])

    def rows_cols(c, r):
        return slice(r * SUB_M, (r + 1) * SUB_M), slice(c * SUB_W, (c + 1) * SUB_W)

    def conv_branch(l2_norm):
        seq_start = (i % tiles_per_seq) == 0
        post = jnp.where(j == COL_Q, HEAD_DIM ** -0.5, 1.0)

        def stage(c, r):
            acc = unit_dot(c, r)
            hidx = j * N_SUB + c
            if r == 0:
                accp_ref[c, 0:CONV_HALO, :] = jnp.where(seq_start, 0.0, halo_ref[hidx])
            accp_ref[c, CONV_HALO + r * SUB_M:CONV_HALO + (r + 1) * SUB_M, :] = acc
            if r == n_halves - 1:
                halo_ref[hidx] = acc[SUB_M - CONV_HALO:SUB_M, :]

        def finish(c, r):
            rows, cols = rows_cols(c, r)
            xp = accp_ref[c, r * SUB_M:r * SUB_M + CONV_HALO + SUB_M, :]
            z = cw_ref[0:1, cols] * xp
            for k in range(1, CONV_WIDTH):
                z = pltpu.roll(z, 1, axis=0) + cw_ref[k:k + 1, cols] * xp
            y = _silu(z[CONV_HALO:, :])
            if l2_norm:
                parts = []
                for hh in range(SUB_W // HEAD_DIM):
                    yh = y[:, hh * HEAD_DIM:(hh + 1) * HEAD_DIM]
                    ss = jnp.sum(yh * yh, axis=-1, keepdims=True)
                    parts.append(yh * (lax.rsqrt(ss + EPS) * post))
                y = jnp.concatenate(parts, axis=1)
            out_ref[rows, cols] = y.astype(BF16)

        pipelined(stage, finish)

    @pl.when(j < COL_V)
    def _():
        conv_branch(True)

    @pl.when(j == COL_V)
    def _():
        conv_branch(False)

    def stage_full(c, r):
        rows, cols = rows_cols(c, r)
        full_ref[rows, cols] = unit_dot(c, r)

    @pl.when((j == COL_ZA) | (j == COL_ZB))
    def _():
        def finish(c, r):
            rows, cols = rows_cols(c, r)
            out_ref[rows, cols] = _silu(full_ref[rows, cols]).astype(BF16)
        pipelined(stage_full, finish)

    @pl.when(j == COL_UB)
    def _():
        for (c, r) in units:
            rows, cols = rows_cols(c, r)
            out_ref[rows, cols] = unit_dot(c, r).astype(BF16)

    @pl.when(j == COL_VB)
    def _():
        def ln_rows(r):
            rows = slice(r * SUB_M, (r + 1) * SUB_M)
            s1 = jnp.zeros((SUB_M, 1), F32)
            for c in range(N_SUB):
                s1 = s1 + jnp.sum(full_ref[rows, c * SUB_W:(c + 1) * SUB_W], axis=-1, keepdims=True)
            mu = s1 * (1.0 / BLOCK_W)
            s2 = jnp.zeros((SUB_M, 1), F32)
            for c in range(N_SUB):
                d = full_ref[rows, c * SUB_W:(c + 1) * SUB_W] - mu
                s2 = s2 + jnp.sum(d * d, axis=-1, keepdims=True)
            rstd = lax.rsqrt(s2 * (1.0 / BLOCK_W) + EPS)
            for c in range(N_SUB):
                cols = slice(c * SUB_W, (c + 1) * SUB_W)
                d = full_ref[rows, cols] - mu
                out_ref[rows, cols] = ((d * rstd) * lnw_ref[:, cols] + lnb_ref[:, cols]).astype(BF16)

        for c in range(N_SUB):
            stage_full(c, 0)
        stage_full(0, 1)
        ln_rows(0)
        for c in range(1, N_SUB):
            stage_full(c, 1)
        ln_rows(1)


def _in_proj(x2, norm_w, w_main, w_gate, conv_w, ln_w, ln_b, seq):
    m, d = x2.shape
    return pl.pallas_call(
        functools.partial(_in_proj_kernel, seq // PROJ_TM),
        grid=(m // PROJ_TM, MAIN_WIDTH // BLOCK_W),
        in_specs=[
            pl.BlockSpec((PROJ_TM, d), lambda i, j: (i, 0)),
            pl.BlockSpec((1, d), lambda i, j: (0, 0)),
            pl.BlockSpec((d, BLOCK_W), lambda i, j: (0, j)),
            pl.BlockSpec((d, 2 * LANES), lambda i, j: (0, 0)),
            pl.BlockSpec((CONV_WIDTH, BLOCK_W), lambda i, j: (0, jnp.minimum(j, COL_V))),
            pl.BlockSpec((1, B_WIDTH), lambda i, j: (0, 0)),
            pl.BlockSpec((1, B_WIDTH), lambda i, j: (0, 0)),
        ],
        out_specs=[
            pl.BlockSpec((PROJ_TM, BLOCK_W), lambda i, j: (i, j)),
            pl.BlockSpec((PROJ_TM, 2 * LANES), lambda i, j: (i, 0)),
        ],
        out_shape=[
            jax.ShapeDtypeStruct((m, MAIN_WIDTH), BF16),
            jax.ShapeDtypeStruct((m, 2 * LANES), F32),
        ],
        scratch_shapes=[
            pltpu.VMEM((PROJ_TM, d), BF16),
            pltpu.VMEM((N_SUB, CONV_HALO + PROJ_TM, SUB_W), F32),
            pltpu.VMEM((3 * N_SUB, CONV_HALO, SUB_W), F32),
            pltpu.VMEM((PROJ_TM, BLOCK_W), F32),
        ],
        compiler_params=pltpu.CompilerParams(
            dimension_semantics=("arbitrary", "arbitrary"), vmem_limit_bytes=VMEM_LIMIT),
        name="in_proj",
    )(x2, norm_w, w_main, w_gate, conv_w, ln_w, ln_b)


def _gdn_kernel(q_ref, k_ref, v_ref, za_ref, gate_ref, alog_ref, dtb_ref, hnw_ref,
                oa_ref, s_ref):
    tt = GDN_TT
    c = CHUNK_A

    @pl.when(pl.program_id(1) == 0)
    def _():
        s_ref[...] = jnp.zeros_like(s_ref)

    gates = gate_ref[...]
    beta = 1.0 / (1.0 + jnp.exp(-gates[:, :LANES]))
    a_in = gates[:, LANES:] + dtb_ref[...]
    softplus = jnp.maximum(a_in, 0.0) + jnp.log(1.0 + jnp.exp(-jnp.abs(a_in)))
    g = -jnp.exp(alog_ref[...]) * softplus

    ri = lax.broadcasted_iota(jnp.int32, (tt, tt), 0)
    ci = lax.broadcasted_iota(jnp.int32, (tt, tt), 1)
    same_chunk = (ri // c) == (ci // c)
    tri = jnp.where(same_chunk & (ci <= ri), 1.0, 0.0).astype(BF16)
    ones_bd = jnp.where(same_chunk, 1.0, 0.0).astype(BF16)
    gc = _mm_split(tri, g)
    gl = _mm_split(ones_bd, g)
    e_gc = jnp.exp(gc)
    e_kd = jnp.exp(gl - gc)
    e_gl = jnp.exp(gl)

    row = lax.broadcasted_iota(jnp.int32, (c, c), 0)
    col = lax.broadcasted_iota(jnp.int32, (c, c), 1)
    incl = row >= col
    strict = row > col
    eye = jnp.where(row == col, 1.0, 0.0).astype(F32)
    hnw = hnw_ref[...]

    n_chunks = tt // c
    pairs = [(n, h) for n in range(n_chunks) for h in range(A_HEADS)]
    gc_t = [gc[n * c:(n + 1) * c, :].T for n in range(n_chunks)]

    def rows_of(n):
        return slice(n * c, (n + 1) * c)

    def head(ref, n, h):
        return ref[rows_of(n), h * LANES:(h + 1) * LANES]

    def colv(arr, n, h):
        return arr[rows_of(n), h:h + 1]

    kb, attn, lmat = {}, {}, {}
    for (n, h) in pairs:
        q, k = head(q_ref, n, h), head(k_ref, n, h)
        kb[n, h] = k.astype(F32) * colv(beta, n, h)
        diff = colv(gc, n, h) - gc_t[n][h:h + 1, :]
        decay = jnp.where(incl, jnp.exp(jnp.where(incl, diff, 0.0)), 0.0)
        kk = _mm_nt(jnp.concatenate([q, kb[n, h]], axis=0), k)
        attn[n, h] = kk[:c] * decay
        lmat[n, h] = jnp.where(strict, kk[c:] * decay, 0.0)
    p = {key: eye - lmat[key] for key in pairs}
    m = {key: _mm(lmat[key], lmat[key]) for key in pairs}
    for it in range(5):
        p = {key: p[key] + _mm(p[key], m[key]) for key in pairs}
        if it < 4:
            m = {key: _mm(m[key], m[key]) for key in pairs}
    u, w = {}, {}
    for (n, h) in pairs:
        vb = head(v_ref, n, h).astype(F32) * colv(beta, n, h)
        kbe = kb[n, h] * colv(e_gc, n, h)
        sol = _mm(p[n, h], jnp.concatenate([vb, kbe], axis=1))
        u[n, h] = sol[:, :HEAD_DIM]
        w[n, h] = sol[:, HEAD_DIM:]

    s = [s_ref[h] for h in range(A_HEADS)]
    for n in range(n_chunks):
        qe = [head(q_ref, n, h).astype(F32) * colv(e_gc, n, h) for h in range(A_HEADS)]
        ws = [_mm(jnp.concatenate([w[n, h], qe[h]], axis=0), s[h]) for h in range(A_HEADS)]
        v_new = [u[n, h] - ws[h][:c] for h in range(A_HEADS)]
        o = [ws[h][c:] + _mm(attn[n, h], v_new[h]) for h in range(A_HEADS)]
        s = [s[h] * e_gl[n * c:n * c + 1, h:h + 1]
             + _mm_tn(head(k_ref, n, h).astype(F32) * colv(e_kd, n, h), v_new[h])
             for h in range(A_HEADS)]
        for h in range(A_HEADS):
            on = (o[h] * lax.rsqrt(jnp.mean(o[h] * o[h], axis=-1, keepdims=True) + EPS)) * hnw
            sz = head(za_ref, n, h).astype(F32)
            oa_ref[rows_of(n), h * LANES:(h + 1) * LANES] = (on * sz).astype(BF16)
    for h in range(A_HEADS):
        s_ref[h] = s[h]


def _gdn(act, gates, alog_lane, dtb_lane, hnw, batch, seq):
    steps = seq // GDN_TT

    def col_block(cb):
        return pl.BlockSpec((GDN_TT, BLOCK_W), lambda b, t: (b * steps + t, cb))

    return pl.pallas_call(
        _gdn_kernel,
        grid=(batch, steps),
        in_specs=[
            col_block(COL_Q), col_block(COL_K), col_block(COL_V), col_block(COL_ZA),
            pl.BlockSpec((GDN_TT, 2 * LANES), lambda b, t: (b * steps + t, 0)),
            pl.BlockSpec((1, LANES), lambda b, t: (0, 0)),
            pl.BlockSpec((1, LANES), lambda b, t: (0, 0)),
            pl.BlockSpec((1, HEAD_DIM), lambda b, t: (0, 0)),
        ],
        out_specs=pl.BlockSpec((GDN_TT, A_WIDTH), lambda b, t: (b * steps + t, 0)),
        out_shape=jax.ShapeDtypeStruct((batch * seq, A_WIDTH), BF16),
        scratch_shapes=[pltpu.VMEM((A_HEADS, HEAD_DIM, HEAD_DIM), F32)],
        compiler_params=pltpu.CompilerParams(
            dimension_semantics=("arbitrary", "arbitrary"), vmem_limit_bytes=VMEM_LIMIT),
        name="gdn",
    )(act, act, act, act, gates, alog_lane, dtb_lane, hnw)


def _sgu_kernel(ub_ref, vn_ref, zb_ref, ws_ref, bs_ref, ob_ref):
    row = lax.broadcasted_iota(jnp.int32, (CHUNK_B, CHUNK_B), 0)
    col = lax.broadcasted_iota(jnp.int32, (CHUNK_B, CHUNK_B), 1)
    causal = row >= col
    bias = bs_ref[...]
    for gidx in range(B_GROUPS):
        cols = slice(gidx * GROUP_DIM, (gidx + 1) * GROUP_DIM)
        w_m = jnp.where(causal, ws_ref[gidx], 0.0).astype(BF16)
        b_col = bias[:, gidx:gidx + 1]
        for n in range(SGU_TT // CHUNK_B):
            rows = slice(n * CHUNK_B, (n + 1) * CHUNK_B)
            s = jnp.dot(w_m, vn_ref[rows, cols], preferred_element_type=F32) + b_col
            u = ub_ref[rows, cols].astype(F32)
            sz = zb_ref[rows, cols].astype(F32)
            ob_ref[rows, cols] = (u * s * sz).astype(BF16)


def _sgu(act, w_spatial, bias_t):
    m = act.shape[0]
    return pl.pallas_call(
        _sgu_kernel,
        grid=(m // SGU_TT,),
        in_specs=[
            pl.BlockSpec((SGU_TT, B_WIDTH), lambda i: (i, COL_UB)),
            pl.BlockSpec((SGU_TT, B_WIDTH), lambda i: (i, COL_VB)),
            pl.BlockSpec((SGU_TT, B_WIDTH), lambda i: (i, COL_ZB)),
            pl.BlockSpec((B_GROUPS, CHUNK_B, CHUNK_B), lambda i: (0, 0, 0)),
            pl.BlockSpec((CHUNK_B, LANES), lambda i: (0, 0)),
        ],
        out_specs=pl.BlockSpec((SGU_TT, B_WIDTH), lambda i: (i, 0)),
        out_shape=jax.ShapeDtypeStruct((m, B_WIDTH), BF16),
        compiler_params=pltpu.CompilerParams(
            dimension_semantics=("arbitrary",), vmem_limit_bytes=VMEM_LIMIT),
        name="sgu",
    )(act, act, act, w_spatial, bias_t)


def _out_proj_kernel(oa_ref, ob_ref, x_ref, wa_ref, wb_ref, fnw_ref, out_ref):
    mix = (jnp.dot(oa_ref[...], wa_ref[...], preferred_element_type=F32)
           + jnp.dot(ob_ref[...], wb_ref[...], preferred_element_type=F32))
    h = x_ref[...] + mix
    ms = jnp.mean(h * h, axis=-1, keepdims=True)
    out_ref[...] = (h * lax.rsqrt(ms + EPS)) * fnw_ref[...]


def _out_proj(o_a, o_b, x2, w_out_bf, final_norm_w):
    m, d = x2.shape
    return pl.pallas_call(
        _out_proj_kernel,
        grid=(m // OUT_TM,),
        in_specs=[
            pl.BlockSpec((OUT_TM, A_WIDTH), lambda i: (i, 0)),
            pl.BlockSpec((OUT_TM, B_WIDTH), lambda i: (i, 0)),
            pl.BlockSpec((OUT_TM, d), lambda i: (i, 0)),
            pl.BlockSpec((A_WIDTH, d), lambda i: (0, 0)),
            pl.BlockSpec((B_WIDTH, d), lambda i: (1, 0)),
            pl.BlockSpec((1, d), lambda i: (0, 0)),
        ],
        out_specs=pl.BlockSpec((OUT_TM, d), lambda i: (i, 0)),
        out_shape=jax.ShapeDtypeStruct((m, d), F32),
        compiler_params=pltpu.CompilerParams(
            dimension_semantics=("arbitrary",), vmem_limit_bytes=VMEM_LIMIT),
        name="out_proj",
    )(o_a, o_b, x2, w_out_bf, w_out_bf, final_norm_w)


def _lane_row(vec):
    return jnp.pad(vec.astype(F32), (0, LANES - vec.shape[0])).reshape(1, LANES)


def kernel(x, norm_w, w_in, conv_w, a_log, dt_bias, head_norm_w, sgu_ln_w, sgu_ln_b,
           w_spatial, b_spatial, w_out, final_norm_w):
    batch, seq, d = x.shape
    assert norm_w.shape[0] == 1, "single-layer problem"
    assert seq % PROJ_TM == 0 and seq % GDN_TT == 0 and seq % SGU_TT == 0
    x2 = x.reshape(batch * seq, d)

    w_bf = w_in[0].astype(BF16)
    gate0 = 4 * A_WIDTH
    w_main = jnp.concatenate([w_bf[:, :gate0], w_bf[:, gate0 + 2 * A_HEADS:]], axis=1)
    w_b = jnp.pad(w_bf[:, gate0:gate0 + A_HEADS], ((0, 0), (0, LANES - A_HEADS)))
    w_a = jnp.pad(w_bf[:, gate0 + A_HEADS:gate0 + 2 * A_HEADS], ((0, 0), (0, LANES - A_HEADS)))
    w_gate = jnp.concatenate([w_b, w_a], axis=1)

    act, gates = _in_proj(x2, norm_w, w_main, w_gate, conv_w[0], sgu_ln_w, sgu_ln_b, seq)
    o_a = _gdn(act, gates, _lane_row(a_log[0]), _lane_row(dt_bias[0]), head_norm_w, batch, seq)
    bias_t = jnp.pad(b_spatial[0].T.astype(F32), ((0, 0), (0, LANES - B_GROUPS)))
    o_b = _sgu(act, w_spatial[0], bias_t)
    out = _out_proj(o_a, o_b, x2, w_out[0].astype(BF16), final_norm_w.reshape(1, d))
    return out.reshape(batch, seq, d)
```

```python
import functools

import jax
import jax.numpy as jnp
from jax import lax
from jax.experimental import pallas as pl
from jax.experimental.pallas import tpu as pltpu

F32 = jnp.float32
BF16 = jnp.bfloat16

EPS = 1e-6
A_HEADS = 8
HEAD_DIM = 128
A_WIDTH = A_HEADS * HEAD_DIM
CONV_WIDTH = 4
CHUNK_A = 64
B_GROUPS = 8
GROUP_DIM = 128
B_WIDTH = B_GROUPS * GROUP_DIM
CHUNK_B = 128
LANES = 128
MXU_WIDTH = 256
CONV_HALO = 8

COL_Q, COL_K, COL_V, COL_ZA, COL_UB, COL_VB, COL_ZB = range(7)
BLOCK_W = 1024
MAIN_WIDTH = 7 * BLOCK_W
SUB_W = MXU_WIDTH
SUB_M = 512
N_SUB = BLOCK_W // SUB_W

PROJ_TM = 1024
GDN_TT = 256
OUT_TM = 512
VMEM_LIMIT = 52 * 1024 * 1024


def _mm(a, b):
    return jnp.dot(a.astype(BF16), b.astype(BF16), preferred_element_type=F32)


def _mm_nt(a, b):
    return lax.dot_general(a.astype(BF16), b.astype(BF16), (((1,), (1,)), ((), ())),
                           preferred_element_type=F32)


def _mm_tn(a, b):
    return lax.dot_general(a.astype(BF16), b.astype(BF16), (((0,), (0,)), ((), ())),
                           preferred_element_type=F32)


def _mm_split(a01, b):
    hi = b.astype(BF16)
    lo = (b - hi.astype(F32)).astype(BF16)
    return (jnp.dot(a01, hi, preferred_element_type=F32)
            + jnp.dot(a01, lo, preferred_element_type=F32))


def _silu(y):
    h = 0.5 * y
    return h + h * jnp.tanh(h)


def _in_proj_kernel(tiles_per_seq, x_ref, nw_ref, w_ref, wg_ref, cw_ref, lnw_ref, lnb_ref,
                    out_ref, gate_ref, xn_ref, accp_ref, halo_ref, full_ref):
    i = pl.program_id(0)
    j = pl.program_id(1)

    @pl.when(j == 0)
    def _():
        x = x_ref[...]
        ms = jnp.mean(x * x, axis=-1, keepdims=True)
        xn = ((x * lax.rsqrt(ms + EPS)) * nw_ref[...]).astype(BF16)
        xn_ref[...] = xn
        gate_ref[...] = _mm_nt(xn, wg_ref[...])

    @pl.when((j == 0) & (i == 0))
    def _():
        halo_ref[...] = jnp.zeros_like(halo_ref)

    n_halves = PROJ_TM // SUB_M
    units = [(c, r) for c in range(N_SUB) for r in range(n_halves)]

    def unit_dot(c, r):
        return _mm_nt(xn_ref[r * SUB_M:(r + 1) * SUB_M, :], w_ref[c * SUB_W:(c + 1) * SUB_W, :])

    def pipelined(stage, finish):
        for n, unit in enumerate(units):
            stage(*unit)
            if n > 0:
                finish(*units[n - 1])
        finish(*units[-1])

    def rows_cols(c, r):
        return slice(r * SUB_M, (r + 1) * SUB_M), slice(c * SUB_W, (c + 1) * SUB_W)

    def conv_branch(l2_norm):
        seq_start = (i % tiles_per_seq) == 0
        post = jnp.where(j == COL_Q, HEAD_DIM ** -0.5, 1.0)

        def stage(c, r):
            acc = unit_dot(c, r)
            hidx = j * N_SUB + c
            if r == 0:
                accp_ref[c, 0:CONV_HALO, :] = jnp.where(seq_start, 0.0, halo_ref[hidx])
            accp_ref[c, CONV_HALO + r * SUB_M:CONV_HALO + (r + 1) * SUB_M, :] = acc
            if r == n_halves - 1:
                halo_ref[hidx] = acc[SUB_M - CONV_HALO:SUB_M, :]

        def finish(c, r):
            rows, cols = rows_cols(c, r)
            xp = accp_ref[c, r * SUB_M:r * SUB_M + CONV_HALO + SUB_M, :]
            x1 = pltpu.roll(xp, 1, axis=0)
            near = cw_ref[3:4, cols] * xp + cw_ref[2:3, cols] * x1
            far = cw_ref[1:2, cols] * xp + cw_ref[0:1, cols] * x1
            z = near + pltpu.roll(far, 2, axis=0)
            y = _silu(z[CONV_HALO:, :])
            if l2_norm:
                parts = []
                for hh in range(SUB_W // HEAD_DIM):
                    yh = y[:, hh * HEAD_DIM:(hh + 1) * HEAD_DIM]
                    ss = jnp.sum(yh * yh, axis=-1, keepdims=True)
                    parts.append(yh * (lax.rsqrt(ss + EPS) * post))
                y = jnp.concatenate(parts, axis=1)
            out_ref[rows, cols] = y.astype(BF16)

        pipelined(stage, finish)

    @pl.when(j < COL_V)
    def _():
        conv_branch(True)

    @pl.when(j == COL_V)
    def _():
        conv_branch(False)

    def stage_full(c, r):
        rows, cols = rows_cols(c, r)
        full_ref[rows, cols] = unit_dot(c, r)

    @pl.when((j == COL_ZA) | (j == COL_ZB))
    def _():
        def finish(c, r):
            rows, cols = rows_cols(c, r)
            out_ref[rows, cols] = _silu(full_ref[rows, cols]).astype(BF16)
        pipelined(stage_full, finish)

    @pl.when(j == COL_UB)
    def _():
        for (c, r) in units:
            rows, cols = rows_cols(c, r)
            out_ref[rows, cols] = unit_dot(c, r).astype(BF16)

    @pl.when(j == COL_VB)
    def _():
        def ln_rows(r):
            rows = slice(r * SUB_M, (r + 1) * SUB_M)
            s1 = jnp.zeros((SUB_M, 1), F32)
            for c in range(N_SUB):
                s1 = s1 + jnp.sum(full_ref[rows, c * SUB_W:(c + 1) * SUB_W], axis=-1, keepdims=True)
            mu = s1 * (1.0 / BLOCK_W)
            s2 = jnp.zeros((SUB_M, 1), F32)
            for c in range(N_SUB):
                d = full_ref[rows, c * SUB_W:(c + 1) * SUB_W] - mu
                s2 = s2 + jnp.sum(d * d, axis=-1, keepdims=True)
            rstd = lax.rsqrt(s2 * (1.0 / BLOCK_W) + EPS)
            for c in range(N_SUB):
                cols = slice(c * SUB_W, (c + 1) * SUB_W)
                d = full_ref[rows, cols] - mu
                out_ref[rows, cols] = ((d * rstd) * lnw_ref[:, cols] + lnb_ref[:, cols]).astype(BF16)

        for c in range(N_SUB):
            stage_full(c, 0)
        stage_full(0, 1)
        ln_rows(0)
        for c in range(1, N_SUB):
            stage_full(c, 1)
        ln_rows(1)


def _in_proj(x2, norm_w, w_t, w_gate_t, conv_w, ln_w, ln_b, seq):
    m, d = x2.shape

    def w_row_start(i, j):
        start = j * BLOCK_W + jnp.where(j > COL_ZA, 2 * A_HEADS, 0)
        return (pl.multiple_of(start, 2 * A_HEADS), 0)

    return pl.pallas_call(
        functools.partial(_in_proj_kernel, seq // PROJ_TM),
        grid=(m // PROJ_TM, MAIN_WIDTH // BLOCK_W),
        in_specs=[
            pl.BlockSpec((PROJ_TM, d), lambda i, j: (i, 0)),
            pl.BlockSpec((1, d), lambda i, j: (0, 0)),
            pl.BlockSpec((pl.Element(BLOCK_W), pl.Element(d)), w_row_start),
            pl.BlockSpec((2 * LANES, d), lambda i, j: (0, 0)),
            pl.BlockSpec((CONV_WIDTH, BLOCK_W), lambda i, j: (0, jnp.minimum(j, COL_V))),
            pl.BlockSpec((1, B_WIDTH), lambda i, j: (0, 0)),
            pl.BlockSpec((1, B_WIDTH), lambda i, j: (0, 0)),
        ],
        out_specs=[
            pl.BlockSpec((PROJ_TM, BLOCK_W), lambda i, j: (i, j)),
            pl.BlockSpec((PROJ_TM, 2 * LANES), lambda i, j: (i, 0)),
        ],
        out_shape=[
            jax.ShapeDtypeStruct((m, MAIN_WIDTH), BF16),
            jax.ShapeDtypeStruct((m, 2 * LANES), F32),
        ],
        scratch_shapes=[
            pltpu.VMEM((PROJ_TM, d), BF16),
            pltpu.VMEM((N_SUB, CONV_HALO + PROJ_TM, SUB_W), F32),
            pltpu.VMEM((3 * N_SUB, CONV_HALO, SUB_W), F32),
            pltpu.VMEM((PROJ_TM, BLOCK_W), F32),
        ],
        compiler_params=pltpu.CompilerParams(
            dimension_semantics=("arbitrary", "arbitrary"), vmem_limit_bytes=VMEM_LIMIT),
        name="in_proj",
    )(x2, norm_w, w_t, w_gate_t, conv_w, ln_w, ln_b)


def _gdn_kernel(q_ref, k_ref, v_ref, za_ref, gate_ref, alog_ref, dtb_ref, hnw_ref,
                oa_ref, s_ref):
    tt = GDN_TT
    c = CHUNK_A

    @pl.when(pl.program_id(1) == 0)
    def _():
        s_ref[...] = jnp.zeros_like(s_ref)

    gates = gate_ref[...]
    beta = 1.0 / (1.0 + jnp.exp(-gates[:, :LANES]))
    a_in = gates[:, LANES:] + dtb_ref[...]
    softplus = jnp.maximum(a_in, 0.0) + jnp.log(1.0 + jnp.exp(-jnp.abs(a_in)))
    g = -jnp.exp(alog_ref[...]) * softplus

    ri = lax.broadcasted_iota(jnp.int32, (tt, tt), 0)
    ci = lax.broadcasted_iota(jnp.int32, (tt, tt), 1)
    same_chunk = (ri // c) == (ci // c)
    tri = jnp.where(same_chunk & (ci <= ri), 1.0, 0.0).astype(BF16)
    ones_bd = jnp.where(same_chunk, 1.0, 0.0).astype(BF16)
    gc = _mm_split(tri, g)
    gl = _mm_split(ones_bd, g)
    e_gc = jnp.exp(gc)
    e_kd = jnp.exp(gl - gc)
    e_gl = jnp.exp(gl)

    row = lax.broadcasted_iota(jnp.int32, (c, c), 0)
    col = lax.broadcasted_iota(jnp.int32, (c, c), 1)
    incl = row >= col
    strict = row > col
    eye = jnp.where(row == col, 1.0, 0.0).astype(F32)
    hnw = hnw_ref[...]

    n_chunks = tt // c
    pairs = [(n, h) for n in range(n_chunks) for h in range(A_HEADS)]
    gc_t = [gc[n * c:(n + 1) * c, :].T for n in range(n_chunks)]

    def rows_of(n):
        return slice(n * c, (n + 1) * c)

    def head(ref, n, h):
        return ref[rows_of(n), h * LANES:(h + 1) * LANES]

    def colv(arr, n, h):
        return arr[rows_of(n), h:h + 1]

    kb, attn, lmat = {}, {}, {}
    for (n, h) in pairs:
        q, k = head(q_ref, n, h), head(k_ref, n, h)
        kb[n, h] = k.astype(F32) * colv(beta, n, h)
        diff = colv(gc, n, h) - gc_t[n][h:h + 1, :]
        decay = jnp.where(incl, jnp.exp(jnp.where(incl, diff, 0.0)), 0.0)
        kk = _mm_nt(jnp.concatenate([q, kb[n, h]], axis=0), k)
        attn[n, h] = kk[:c] * decay
        lmat[n, h] = jnp.where(strict, kk[c:] * decay, 0.0)
    p = {key: eye - lmat[key] for key in pairs}
    m = {key: _mm(lmat[key], lmat[key]) for key in pairs}
    for it in range(5):
        p = {key: p[key] + _mm(p[key], m[key]) for key in pairs}
        if it < 4:
            m = {key: _mm(m[key], m[key]) for key in pairs}
    u, w = {}, {}
    for (n, h) in pairs:
        vb = head(v_ref, n, h).astype(F32) * colv(beta, n, h)
        kbe = kb[n, h] * colv(e_gc, n, h)
        sol = _mm(p[n, h], jnp.concatenate([vb, kbe], axis=1))
        u[n, h] = sol[:, :HEAD_DIM]
        w[n, h] = sol[:, HEAD_DIM:]

    s = [s_ref[h] for h in range(A_HEADS)]
    for n in range(n_chunks):
        qe = [head(q_ref, n, h).astype(F32) * colv(e_gc, n, h) for h in range(A_HEADS)]
        ws = [_mm(jnp.concatenate([w[n, h], qe[h]], axis=0), s[h]) for h in range(A_HEADS)]
        v_new = [u[n, h] - ws[h][:c] for h in range(A_HEADS)]
        o = [ws[h][c:] + _mm(attn[n, h], v_new[h]) for h in range(A_HEADS)]
        s = [s[h] * e_gl[n * c:n * c + 1, h:h + 1]
             + _mm_tn(head(k_ref, n, h).astype(F32) * colv(e_kd, n, h), v_new[h])
             for h in range(A_HEADS)]
        for h in range(A_HEADS):
            on = (o[h] * lax.rsqrt(jnp.mean(o[h] * o[h], axis=-1, keepdims=True) + EPS)) * hnw
            sz = head(za_ref, n, h).astype(F32)
            oa_ref[rows_of(n), h * LANES:(h + 1) * LANES] = (on * sz).astype(BF16)
    for h in range(A_HEADS):
        s_ref[h] = s[h]


def _gdn(act, gates, alog_lane, dtb_lane, hnw, batch, seq):
    steps = seq // GDN_TT

    def col_block(cb):
        return pl.BlockSpec((GDN_TT, BLOCK_W), lambda b, t: (b * steps + t, cb))

    return pl.pallas_call(
        _gdn_kernel,
        grid=(batch, steps),
        in_specs=[
            col_block(COL_Q), col_block(COL_K), col_block(COL_V), col_block(COL_ZA),
            pl.BlockSpec((GDN_TT, 2 * LANES), lambda b, t: (b * steps + t, 0)),
            pl.BlockSpec((1, LANES), lambda b, t: (0, 0)),
            pl.BlockSpec((1, LANES), lambda b, t: (0, 0)),
            pl.BlockSpec((1, HEAD_DIM), lambda b, t: (0, 0)),
        ],
        out_specs=pl.BlockSpec((GDN_TT, A_WIDTH), lambda b, t: (b * steps + t, 0)),
        out_shape=jax.ShapeDtypeStruct((batch * seq, A_WIDTH), BF16),
        scratch_shapes=[pltpu.VMEM((A_HEADS, HEAD_DIM, HEAD_DIM), F32)],
        compiler_params=pltpu.CompilerParams(
            dimension_semantics=("arbitrary", "arbitrary"), vmem_limit_bytes=VMEM_LIMIT),
        name="gdn",
    )(act, act, act, act, gates, alog_lane, dtb_lane, hnw)


def _out_proj_kernel(oa_ref, ub_ref, vn_ref, zb_ref, x_ref, ws_ref, bs_ref, wa_ref, wb_ref,
                     fnw_ref, out_ref, ob_ref):
    row = lax.broadcasted_iota(jnp.int32, (CHUNK_B, CHUNK_B), 0)
    col = lax.broadcasted_iota(jnp.int32, (CHUNK_B, CHUNK_B), 1)
    causal = row >= col
    bias = bs_ref[...]
    for gidx in range(B_GROUPS):
        cols = slice(gidx * GROUP_DIM, (gidx + 1) * GROUP_DIM)
        w_m = jnp.where(causal, ws_ref[gidx], 0.0).astype(BF16)
        b_col = bias[:, gidx:gidx + 1]
        for n in range(OUT_TM // CHUNK_B):
            rows = slice(n * CHUNK_B, (n + 1) * CHUNK_B)
            s = jnp.dot(w_m, vn_ref[rows, cols], preferred_element_type=F32) + b_col
            u = ub_ref[rows, cols].astype(F32)
            sz = zb_ref[rows, cols].astype(F32)
            ob_ref[rows, cols] = (u * s * sz).astype(BF16)

    mix = (jnp.dot(oa_ref[...], wa_ref[...], preferred_element_type=F32)
           + jnp.dot(ob_ref[...], wb_ref[...], preferred_element_type=F32))
    h = x_ref[...] + mix
    ms = jnp.mean(h * h, axis=-1, keepdims=True)
    out_ref[...] = (h * lax.rsqrt(ms + EPS)) * fnw_ref[...]


def _out_proj(o_a, act, x2, w_spatial, bias_t, w_out_bf, final_norm_w):
    m, d = x2.shape

    def col_block(cb):
        return pl.BlockSpec((OUT_TM, BLOCK_W), lambda i: (i, cb))

    return pl.pallas_call(
        _out_proj_kernel,
        grid=(m // OUT_TM,),
        in_specs=[
            pl.BlockSpec((OUT_TM, A_WIDTH), lambda i: (i, 0)),
            col_block(COL_UB), col_block(COL_VB), col_block(COL_ZB),
            pl.BlockSpec((OUT_TM, d), lambda i: (i, 0)),
            pl.BlockSpec((B_GROUPS, CHUNK_B, CHUNK_B), lambda i: (0, 0, 0)),
            pl.BlockSpec((CHUNK_B, LANES), lambda i: (0, 0)),
            pl.BlockSpec((A_WIDTH, d), lambda i: (0, 0)),
            pl.BlockSpec((B_WIDTH, d), lambda i: (1, 0)),
            pl.BlockSpec((1, d), lambda i: (0, 0)),
        ],
        out_specs=pl.BlockSpec((OUT_TM, d), lambda i: (i, 0)),
        out_shape=jax.ShapeDtypeStruct((m, d), F32),
        scratch_shapes=[pltpu.VMEM((OUT_TM, B_WIDTH), BF16)],
        compiler_params=pltpu.CompilerParams(
            dimension_semantics=("arbitrary",), vmem_limit_bytes=VMEM_LIMIT),
        name="out_proj",
    )(o_a, act, act, act, x2, w_spatial, bias_t, w_out_bf, w_out_bf, final_norm_w)


def _lane_row(vec):
    return jnp.pad(vec.astype(F32), (0, LANES - vec.shape[0])).reshape(1, LANES)


def kernel(x, norm_w, w_in, conv_w, a_log, dt_bias, head_norm_w, sgu_ln_w, sgu_ln_b,
           w_spatial, b_spatial, w_out, final_norm_w):
    batch, seq, d = x.shape
    assert norm_w.shape[0] == 1, "single-layer problem"
    assert seq % PROJ_TM == 0 and seq % GDN_TT == 0 and (batch * seq) % OUT_TM == 0
    x2 = x.reshape(batch * seq, d)

    w_t = jnp.swapaxes(w_in[0], 0, 1).astype(BF16)
    gate0 = (COL_ZA + 1) * BLOCK_W
    zero_rows = jnp.zeros((LANES - A_HEADS, d), BF16)
    w_gate_t = jnp.concatenate([w_t[gate0:gate0 + A_HEADS], zero_rows,
                                w_t[gate0 + A_HEADS:gate0 + 2 * A_HEADS], zero_rows], axis=0)

    act, gates = _in_proj(x2, norm_w, w_t, w_gate_t, conv_w[0], sgu_ln_w, sgu_ln_b, seq)
    o_a = _gdn(act, gates, _lane_row(a_log[0]), _lane_row(dt_bias[0]), head_norm_w, batch, seq)
    bias_t = jnp.pad(b_spatial[0].T.astype(F32), ((0, 0), (0, LANES - B_GROUPS)))
    out = _out_proj(o_a, act, x2, w_spatial[0], bias_t, w_out[0].astype(BF16),
                    final_norm_w.reshape(1, d))
    return out.reshape(batch, seq, d)
```

```python
import functools

import jax
import jax.numpy as jnp
from jax import lax
from jax.experimental import pallas as pl
from jax.experimental.pallas import tpu as pltpu

F32 = jnp.float32
BF16 = jnp.bfloat16

EPS = 1e-6
A_HEADS = 8
HEAD_DIM = 128
A_WIDTH = A_HEADS * HEAD_DIM
CONV_WIDTH = 4
CHUNK_A = 64
B_GROUPS = 8
GROUP_DIM = 128
B_WIDTH = B_GROUPS * GROUP_DIM
CHUNK_B = 128
LANES = 128
MXU_WIDTH = 256
CONV_HALO = 8

COL_Q, COL_K, COL_V, COL_ZA, COL_UB, COL_VB, COL_ZB = range(7)
BLOCK_W = 1024
MAIN_WIDTH = 7 * BLOCK_W
SUB_W = MXU_WIDTH
SUB_M = 512
N_SUB = BLOCK_W // SUB_W

PROJ_TM = 1024
GDN_TT = 256
GDN_NB = 2
OUT_TM = 512
VMEM_LIMIT = 52 * 1024 * 1024


def _mm(a, b):
    return jnp.dot(a.astype(BF16), b.astype(BF16), preferred_element_type=F32)


def _mm_nt(a, b):
    return lax.dot_general(a.astype(BF16), b.astype(BF16), (((1,), (1,)), ((), ())),
                           preferred_element_type=F32)


def _mm_tn(a, b):
    return lax.dot_general(a.astype(BF16), b.astype(BF16), (((0,), (0,)), ((), ())),
                           preferred_element_type=F32)


def _mm_split(a01, b):
    hi = b.astype(BF16)
    lo = (b - hi.astype(F32)).astype(BF16)
    return (jnp.dot(a01, hi, preferred_element_type=F32)
            + jnp.dot(a01, lo, preferred_element_type=F32))


def _silu(y):
    h = 0.5 * y
    return h + h * jnp.tanh(h)


def _in_proj_kernel(tiles_per_seq, x_ref, nw_ref, w_ref, wg_ref, cw_ref, lnw_ref, lnb_ref,
                    out_ref, gate_ref, xn_ref, accp_ref, halo_ref, full_ref):
    i = pl.program_id(0)
    j = pl.program_id(1)

    @pl.when(j == 0)
    def _():
        x = x_ref[...]
        ms = jnp.mean(x * x, axis=-1, keepdims=True)
        xn = ((x * lax.rsqrt(ms + EPS)) * nw_ref[...]).astype(BF16)
        xn_ref[...] = xn
        gate_ref[...] = _mm_nt(xn, wg_ref[...])

    @pl.when((j == 0) & (i == 0))
    def _():
        halo_ref[...] = jnp.zeros_like(halo_ref)

    n_halves = PROJ_TM // SUB_M
    units = [(c, r) for c in range(N_SUB) for r in range(n_halves)]

    def unit_dot(c, r):
        return _mm_nt(xn_ref[r * SUB_M:(r + 1) * SUB_M, :], w_ref[c * SUB_W:(c + 1) * SUB_W, :])

    def pipelined(stage, finish):
        for n, unit in enumerate(units):
            stage(*unit)
            if n > 0:
                finish(*units[n - 1])
        finish(*units[-1])

    def rows_cols(c, r):
        return slice(r * SUB_M, (r + 1) * SUB_M), slice(c * SUB_W, (c + 1) * SUB_W)

    def conv_branch(l2_norm):
        seq_start = (i % tiles_per_seq) == 0
        post = jnp.where(j == COL_Q, HEAD_DIM ** -0.5, 1.0)

        def stage(c, r):
            acc = unit_dot(c, r)
            hidx = j * N_SUB + c
            if r == 0:
                accp_ref[c, 0:CONV_HALO, :] = jnp.where(seq_start, 0.0, halo_ref[hidx])
            accp_ref[c, CONV_HALO + r * SUB_M:CONV_HALO + (r + 1) * SUB_M, :] = acc
            if r == n_halves - 1:
                halo_ref[hidx] = acc[SUB_M - CONV_HALO:SUB_M, :]

        def finish(c, r):
            rows, cols = rows_cols(c, r)
            xp = accp_ref[c, r * SUB_M:r * SUB_M + CONV_HALO + SUB_M, :]
            x1 = pltpu.roll(xp, 1, axis=0)
            near = cw_ref[3:4, cols] * xp + cw_ref[2:3, cols] * x1
            far = cw_ref[1:2, cols] * xp + cw_ref[0:1, cols] * x1
            z = near + pltpu.roll(far, 2, axis=0)
            y = _silu(z[CONV_HALO:, :])
            if l2_norm:
                parts = []
                for hh in range(SUB_W // HEAD_DIM):
                    yh = y[:, hh * HEAD_DIM:(hh + 1) * HEAD_DIM]
                    ss = jnp.sum(yh * yh, axis=-1, keepdims=True)
                    parts.append(yh * (lax.rsqrt(ss + EPS) * post))
                y = jnp.concatenate(parts, axis=1)
            out_ref[rows, cols] = y.astype(BF16)

        pipelined(stage, finish)

    @pl.when(j < COL_V)
    def _():
        conv_branch(True)

    @pl.when(j == COL_V)
    def _():
        conv_branch(False)

    def stage_full(c, r):
        rows, cols = rows_cols(c, r)
        full_ref[rows, cols] = unit_dot(c, r)

    @pl.when((j == COL_ZA) | (j == COL_ZB))
    def _():
        def finish(c, r):
            rows, cols = rows_cols(c, r)
            out_ref[rows, cols] = _silu(full_ref[rows, cols]).astype(BF16)
        pipelined(stage_full, finish)

    @pl.when(j == COL_UB)
    def _():
        for (c, r) in units:
            rows, cols = rows_cols(c, r)
            out_ref[rows, cols] = unit_dot(c, r).astype(BF16)

    @pl.when(j == COL_VB)
    def _():
        def ln_rows(r):
            rows = slice(r * SUB_M, (r + 1) * SUB_M)
            s1 = jnp.zeros((SUB_M, 1), F32)
            for c in range(N_SUB):
                s1 = s1 + jnp.sum(full_ref[rows, c * SUB_W:(c + 1) * SUB_W], axis=-1, keepdims=True)
            mu = s1 * (1.0 / BLOCK_W)
            s2 = jnp.zeros((SUB_M, 1), F32)
            for c in range(N_SUB):
                d = full_ref[rows, c * SUB_W:(c + 1) * SUB_W] - mu
                s2 = s2 + jnp.sum(d * d, axis=-1, keepdims=True)
            rstd = lax.rsqrt(s2 * (1.0 / BLOCK_W) + EPS)
            for c in range(N_SUB):
                cols = slice(c * SUB_W, (c + 1) * SUB_W)
                d = full_ref[rows, cols] - mu
                out_ref[rows, cols] = ((d * rstd) * lnw_ref[:, cols] + lnb_ref[:, cols]).astype(BF16)

        for c in range(N_SUB):
            stage_full(c, 0)
        stage_full(0, 1)
        ln_rows(0)
        for c in range(1, N_SUB):
            stage_full(c, 1)
        ln_rows(1)


def _in_proj(x2, norm_w, w_t, w_gate_t, conv_w, ln_w, ln_b, seq):
    m, d = x2.shape

    def w_row_start(i, j):
        start = j * BLOCK_W + jnp.where(j > COL_ZA, 2 * A_HEADS, 0)
        return (pl.multiple_of(start, 2 * A_HEADS), 0)

    return pl.pallas_call(
        functools.partial(_in_proj_kernel, seq // PROJ_TM),
        grid=(m // PROJ_TM, MAIN_WIDTH // BLOCK_W),
        in_specs=[
            pl.BlockSpec((PROJ_TM, d), lambda i, j: (i, 0)),
            pl.BlockSpec((1, d), lambda i, j: (0, 0)),
            pl.BlockSpec((pl.Element(BLOCK_W), pl.Element(d)), w_row_start),
            pl.BlockSpec((2 * LANES, d), lambda i, j: (0, 0)),
            pl.BlockSpec((CONV_WIDTH, BLOCK_W), lambda i, j: (0, jnp.minimum(j, COL_V))),
            pl.BlockSpec((1, B_WIDTH), lambda i, j: (0, 0)),
            pl.BlockSpec((1, B_WIDTH), lambda i, j: (0, 0)),
        ],
        out_specs=[
            pl.BlockSpec((PROJ_TM, BLOCK_W), lambda i, j: (i, j)),
            pl.BlockSpec((PROJ_TM, 2 * LANES), lambda i, j: (i, 0)),
        ],
        out_shape=[
            jax.ShapeDtypeStruct((m, MAIN_WIDTH), BF16),
            jax.ShapeDtypeStruct((m, 2 * LANES), F32),
        ],
        scratch_shapes=[
            pltpu.VMEM((PROJ_TM, d), BF16),
            pltpu.VMEM((N_SUB, CONV_HALO + PROJ_TM, SUB_W), F32),
            pltpu.VMEM((3 * N_SUB, CONV_HALO, SUB_W), F32),
            pltpu.VMEM((PROJ_TM, BLOCK_W), F32),
        ],
        compiler_params=pltpu.CompilerParams(
            dimension_semantics=("arbitrary", "arbitrary"), vmem_limit_bytes=VMEM_LIMIT),
        name="in_proj",
    )(x2, norm_w, w_t, w_gate_t, conv_w, ln_w, ln_b)


def _gdn_kernel(q_ref, k_ref, v_ref, za_ref, gate_ref, alog_ref, dtb_ref, hnw_ref,
                oa_ref, s_ref):
    tt = GDN_TT
    c = CHUNK_A

    @pl.when(pl.program_id(1) == 0)
    def _():
        s_ref[...] = jnp.zeros_like(s_ref)

    ri = lax.broadcasted_iota(jnp.int32, (tt, tt), 0)
    ci = lax.broadcasted_iota(jnp.int32, (tt, tt), 1)
    same_chunk = (ri // c) == (ci // c)
    tri = jnp.where(same_chunk & (ci <= ri), 1.0, 0.0).astype(BF16)
    ones_bd = jnp.where(same_chunk, 1.0, 0.0).astype(BF16)

    beta, g, e_gc, e_kd, e_gl = [], [], [], [], []
    for bi in range(GDN_NB):
        gates = gate_ref[bi]
        beta.append(1.0 / (1.0 + jnp.exp(-gates[:, :LANES])))
        a_in = gates[:, LANES:] + dtb_ref[...]
        softplus = jnp.maximum(a_in, 0.0) + jnp.log(1.0 + jnp.exp(-jnp.abs(a_in)))
        g.append(-jnp.exp(alog_ref[...]) * softplus)
        gc = _mm_split(tri, g[bi])
        gl = _mm_split(ones_bd, g[bi])
        e_gc.append(jnp.exp(gc))
        e_kd.append(jnp.exp(gl - gc))
        e_gl.append(jnp.exp(gl))

    n_chunks = tt // c
    pairs_per_row = A_HEADS // 2
    n_heads = GDN_NB * A_HEADS
    n_pairs = n_heads // 2
    n_quads = n_heads // 4
    row_q = lax.broadcasted_iota(jnp.int32, (c, 4 * c), 0)
    col_q = lax.broadcasted_iota(jnp.int32, (c, 4 * c), 1) % c
    incl = row_q >= col_q
    strict = row_q > col_q
    eye = jnp.where(row_q == col_q, 1.0, 0.0).astype(F32)
    low_half = lax.broadcasted_iota(jnp.int32, (c, LANES), 1) < c
    r64 = lax.broadcasted_iota(jnp.int32, (c, c), 0)
    c64 = lax.broadcasted_iota(jnp.int32, (c, c), 1)
    tri_c = jnp.where(r64 >= c64, 1.0, 0.0).astype(BF16)
    zeros_blk = jnp.zeros((c, LANES), BF16)
    hnw = hnw_ref[...]

    def rows_of(n):
        return slice(n * c, (n + 1) * c)

    def lanes_of(h, width=LANES):
        return slice(h * width, (h + 1) * width)

    def bcast(arrs, n, h):
        col = h % A_HEADS
        return jnp.broadcast_to(arrs[h // A_HEADS][rows_of(n), col:col + 1], (c, LANES))

    def pair_block(ref, n, p):
        return ref[p // pairs_per_row, rows_of(n), lanes_of(p % pairs_per_row, 2 * LANES)]

    def chunk_decay(n, h):
        col = h % A_HEADS
        return e_gl[h // A_HEADS][n * c:n * c + 1, col:col + 1]

    def pair_bcast(arr, n, p):
        return jnp.concatenate([bcast(arr, n, 2 * p), bcast(arr, n, 2 * p + 1)], axis=1)

    def block_diag_pair(x_pair):
        xb = x_pair.astype(BF16)
        zero = jnp.zeros((xb.shape[0], LANES), BF16)
        top = jnp.concatenate([xb[:, :LANES], zero], axis=1)
        bot = jnp.concatenate([zero, xb[:, LANES:]], axis=1)
        return jnp.concatenate([top, bot], axis=0)

    def block_diag_quad(m_quad):
        mb = m_quad.astype(BF16)
        left, right = mb[:, :LANES], mb[:, LANES:]
        zero = jnp.zeros_like(left)
        blocks = [
            jnp.concatenate([jnp.where(low_half, left, zero), zeros_blk], axis=1),
            jnp.concatenate([jnp.where(low_half, zero, left), zeros_blk], axis=1),
            jnp.concatenate([zeros_blk, jnp.where(low_half, right, zero)], axis=1),
            jnp.concatenate([zeros_blk, jnp.where(low_half, zero, right)], axis=1),
        ]
        return jnp.concatenate(blocks, axis=0)

    chunk_pairs = [(n, p) for n in range(n_chunks) for p in range(n_pairs)]
    chunk_quads = [(n, qd) for n in range(n_chunks) for qd in range(n_quads)]

    kb, kk = {}, {}
    for (n, p) in chunk_pairs:
        k_pair = pair_block(k_ref, n, p)
        kb[n, p] = k_pair.astype(F32) * pair_bcast(beta, n, p)
        lhs = jnp.concatenate([pair_block(q_ref, n, p), kb[n, p].astype(BF16)], axis=0)
        kk[n, p] = _mm_nt(lhs, block_diag_pair(k_pair))

    attn, p_inv, m_pow = {}, {}, {}
    for (n, qd) in chunk_quads:
        gb = [bcast(g, n, 4 * qd + hh) for hh in range(4)]
        g_quad = jnp.concatenate([jnp.where(low_half, gb[0], gb[1]),
                                  jnp.where(low_half, gb[2], gb[3])], axis=1)
        diff = _mm_split(tri_c, jnp.where(strict, g_quad, 0.0))
        decay = jnp.where(incl, jnp.exp(jnp.where(incl, diff, 0.0)), 0.0)
        qk = jnp.concatenate([kk[n, 2 * qd][:c], kk[n, 2 * qd + 1][:c]], axis=1)
        kbk = jnp.concatenate([kk[n, 2 * qd][c:], kk[n, 2 * qd + 1][c:]], axis=1)
        attn[n, qd] = qk * decay
        lmat = jnp.where(strict, kbk * decay, 0.0)
        p_inv[n, qd] = eye - lmat
        m_pow[n, qd] = lmat
    m_pow = {key: _mm(m_pow[key], block_diag_quad(m_pow[key])) for key in chunk_quads}
    for it in range(5):
        bd = {key: block_diag_quad(m_pow[key]) for key in chunk_quads}
        if it < 4:
            both = {key: _mm(jnp.concatenate([p_inv[key], m_pow[key]], axis=0), bd[key])
                    for key in chunk_quads}
            p_inv = {key: p_inv[key] + both[key][:c] for key in chunk_quads}
            m_pow = {key: both[key][c:] for key in chunk_quads}
        else:
            p_inv = {key: p_inv[key] + _mm(p_inv[key], bd[key]) for key in chunk_quads}

    u, w = {}, {}
    for (n, p) in chunk_pairs:
        vb = pair_block(v_ref, n, p).astype(F32) * pair_bcast(beta, n, p)
        kbe = kb[n, p] * pair_bcast(e_gc, n, p)
        rhs = jnp.concatenate([block_diag_pair(vb), block_diag_pair(kbe)], axis=1)
        sol = _mm(p_inv[n, p // 2][:, lanes_of(p % 2)], rhs)
        u[n, p] = sol[:, :2 * LANES]
        w[n, p] = sol[:, 2 * LANES:]

    kdec_t = {}
    for (n, p) in chunk_pairs:
        kdec = pair_block(k_ref, n, p).astype(F32) * pair_bcast(e_kd, n, p)
        kdec_t[n, p] = jnp.concatenate([kdec[:, :LANES], kdec[:, LANES:]], axis=0).T

    s = [s_ref[p] for p in range(n_pairs)]
    for n in range(n_chunks):
        qe = [pair_block(q_ref, n, p).astype(F32) * pair_bcast(e_gc, n, p) for p in range(n_pairs)]
        ws = [_mm(jnp.concatenate([w[n, p], qe[p]], axis=0), block_diag_pair(s[p]))
              for p in range(n_pairs)]
        v_new = [u[n, p] - ws[p][:c] for p in range(n_pairs)]
        mixed = [_mm(jnp.concatenate([attn[n, p // 2][:, lanes_of(p % 2)].astype(BF16),
                                      kdec_t[n, p].astype(BF16)], axis=0),
                     block_diag_pair(v_new[p])) for p in range(n_pairs)]
        o = [ws[p][c:] + mixed[p][:c] for p in range(n_pairs)]
        s = [jnp.concatenate([s[p][:, :LANES] * chunk_decay(n, 2 * p),
                              s[p][:, LANES:] * chunk_decay(n, 2 * p + 1)], axis=1)
             + mixed[p][c:] for p in range(n_pairs)]
        for h in range(n_heads):
            oh = o[h // 2][:, lanes_of(h % 2)]
            on = (oh * lax.rsqrt(jnp.mean(oh * oh, axis=-1, keepdims=True) + EPS)) * hnw
            bi, col = h // A_HEADS, h % A_HEADS
            sz = za_ref[bi, rows_of(n), lanes_of(col)].astype(F32)
            oa_ref[bi, rows_of(n), lanes_of(col)] = (on * sz).astype(BF16)
    for p in range(n_pairs):
        s_ref[p] = s[p]


def _gdn(act, gates, alog_lane, dtb_lane, hnw, batch, seq):
    def col_block(cb):
        return pl.BlockSpec((GDN_NB, GDN_TT, BLOCK_W), lambda b, t: (b, t, cb))

    return pl.pallas_call(
        _gdn_kernel,
        grid=(batch // GDN_NB, seq // GDN_TT),
        in_specs=[
            col_block(COL_Q), col_block(COL_K), col_block(COL_V), col_block(COL_ZA),
            pl.BlockSpec((GDN_NB, GDN_TT, 2 * LANES), lambda b, t: (b, t, 0)),
            pl.BlockSpec((1, LANES), lambda b, t: (0, 0)),
            pl.BlockSpec((1, LANES), lambda b, t: (0, 0)),
            pl.BlockSpec((1, HEAD_DIM), lambda b, t: (0, 0)),
        ],
        out_specs=pl.BlockSpec((GDN_NB, GDN_TT, A_WIDTH), lambda b, t: (b, t, 0)),
        out_shape=jax.ShapeDtypeStruct((batch, seq, A_WIDTH), BF16),
        scratch_shapes=[pltpu.VMEM((GDN_NB * A_HEADS // 2, HEAD_DIM, 2 * HEAD_DIM), F32)],
        compiler_params=pltpu.CompilerParams(
            dimension_semantics=("arbitrary", "arbitrary"), vmem_limit_bytes=VMEM_LIMIT),
        name="gdn",
    )(act, act, act, act, gates, alog_lane, dtb_lane, hnw)


def _out_proj_kernel(oa_ref, ub_ref, vn_ref, zb_ref, x_ref, ws_ref, bs_ref, wa_ref, wb_ref,
                     fnw_ref, out_ref, ob_ref):
    row = lax.broadcasted_iota(jnp.int32, (CHUNK_B, CHUNK_B), 0)
    col = lax.broadcasted_iota(jnp.int32, (CHUNK_B, CHUNK_B), 1)
    causal = row >= col
    bias = bs_ref[...]
    for gidx in range(B_GROUPS):
        cols = slice(gidx * GROUP_DIM, (gidx + 1) * GROUP_DIM)
        w_m = jnp.where(causal, ws_ref[gidx], 0.0).astype(BF16)
        b_col = bias[:, gidx:gidx + 1]
        for n in range(OUT_TM // CHUNK_B):
            rows = slice(n * CHUNK_B, (n + 1) * CHUNK_B)
            s = jnp.dot(w_m, vn_ref[rows, cols], preferred_element_type=F32) + b_col
            u = ub_ref[rows, cols].astype(F32)
            sz = zb_ref[rows, cols].astype(F32)
            ob_ref[rows, cols] = (u * s * sz).astype(BF16)

    mix = (jnp.dot(oa_ref[...], wa_ref[...], preferred_element_type=F32)
           + jnp.dot(ob_ref[...], wb_ref[...], preferred_element_type=F32))
    h = x_ref[...] + mix
    ms = jnp.mean(h * h, axis=-1, keepdims=True)
    out_ref[...] = (h * lax.rsqrt(ms + EPS)) * fnw_ref[...]


def _out_proj(o_a, act, x2, w_spatial, bias_t, w_out_bf, final_norm_w):
    m, d = x2.shape

    def col_block(cb):
        return pl.BlockSpec((OUT_TM, BLOCK_W), lambda i: (i, cb))

    return pl.pallas_call(
        _out_proj_kernel,
        grid=(m // OUT_TM,),
        in_specs=[
            pl.BlockSpec((OUT_TM, A_WIDTH), lambda i: (i, 0)),
            col_block(COL_UB), col_block(COL_VB), col_block(COL_ZB),
            pl.BlockSpec((OUT_TM, d), lambda i: (i, 0)),
            pl.BlockSpec((B_GROUPS, CHUNK_B, CHUNK_B), lambda i: (0, 0, 0)),
            pl.BlockSpec((CHUNK_B, LANES), lambda i: (0, 0)),
            pl.BlockSpec((A_WIDTH, d), lambda i: (0, 0)),
            pl.BlockSpec((B_WIDTH, d), lambda i: (1, 0)),
            pl.BlockSpec((1, d), lambda i: (0, 0)),
        ],
        out_specs=pl.BlockSpec((OUT_TM, d), lambda i: (i, 0)),
        out_shape=jax.ShapeDtypeStruct((m, d), F32),
        scratch_shapes=[pltpu.VMEM((OUT_TM, B_WIDTH), BF16)],
        compiler_params=pltpu.CompilerParams(
            dimension_semantics=("arbitrary",), vmem_limit_bytes=VMEM_LIMIT),
        name="out_proj",
    )(o_a, act, act, act, x2, w_spatial, bias_t, w_out_bf, w_out_bf, final_norm_w)


def _lane_row(vec):
    return jnp.pad(vec.astype(F32), (0, LANES - vec.shape[0])).reshape(1, LANES)


def kernel(x, norm_w, w_in, conv_w, a_log, dt_bias, head_norm_w, sgu_ln_w, sgu_ln_b,
           w_spatial, b_spatial, w_out, final_norm_w):
    batch, seq, d = x.shape
    assert norm_w.shape[0] == 1, "single-layer problem"
    assert seq % PROJ_TM == 0 and seq % GDN_TT == 0 and (batch * seq) % OUT_TM == 0
    assert batch % GDN_NB == 0
    x2 = x.reshape(batch * seq, d)

    w_t = jnp.swapaxes(w_in[0], 0, 1).astype(BF16)
    gate0 = (COL_ZA + 1) * BLOCK_W
    zero_rows = jnp.zeros((LANES - A_HEADS, d), BF16)
    w_gate_t = jnp.concatenate([w_t[gate0:gate0 + A_HEADS], zero_rows,
                                w_t[gate0 + A_HEADS:gate0 + 2 * A_HEADS], zero_rows], axis=0)

    act, gates = _in_proj(x2, norm_w, w_t, w_gate_t, conv_w[0], sgu_ln_w, sgu_ln_b, seq)
    o_a = _gdn(act.reshape(batch, seq, MAIN_WIDTH), gates.reshape(batch, seq, 2 * LANES),
               _lane_row(a_log[0]), _lane_row(dt_bias[0]), head_norm_w, batch, seq)
    o_a = o_a.reshape(batch * seq, A_WIDTH)
    bias_t = jnp.pad(b_spatial[0].T.astype(F32), ((0, 0), (0, LANES - B_GROUPS)))
    out = _out_proj(o_a, act, x2, w_spatial[0], bias_t, w_out[0].astype(BF16),
                    final_norm_w.reshape(1, d))
    return out.reshape(batch, seq, d)
```

```python
import functools

import jax
import jax.numpy as jnp
from jax import lax
from jax.experimental import pallas as pl
from jax.experimental.pallas import tpu as pltpu

F32 = jnp.float32
BF16 = jnp.bfloat16

EPS = 1e-6
A_HEADS = 8
HEAD_DIM = 128
A_WIDTH = A_HEADS * HEAD_DIM
CONV_WIDTH = 4
CHUNK_A = 64
B_GROUPS = 8
GROUP_DIM = 128
B_WIDTH = B_GROUPS * GROUP_DIM
CHUNK_B = 128
LANES = 128
MXU_WIDTH = 256
CONV_HALO = 8

COL_Q, COL_K, COL_V, COL_ZA, COL_UB, COL_VB, COL_ZB = range(7)
BLOCK_W = 1024
MAIN_WIDTH = 7 * BLOCK_W
SUB_W = MXU_WIDTH
SUB_M = 256
N_SUB = BLOCK_W // SUB_W

PROJ_TM = 1024
GDN_TT = 256
GDN_NB = 2
OUT_TM = 512
VMEM_LIMIT = 52 * 1024 * 1024


def _mm(a, b):
    return jnp.dot(a.astype(BF16), b.astype(BF16), preferred_element_type=F32)


def _mm_nt(a, b):
    return lax.dot_general(a.astype(BF16), b.astype(BF16), (((1,), (1,)), ((), ())),
                           preferred_element_type=F32)


def _mm_tn(a, b):
    return lax.dot_general(a.astype(BF16), b.astype(BF16), (((0,), (0,)), ((), ())),
                           preferred_element_type=F32)


def _mm_split(a01, b):
    hi = b.astype(BF16)
    lo = (b - hi.astype(F32)).astype(BF16)
    return (jnp.dot(a01, hi, preferred_element_type=F32)
            + jnp.dot(a01, lo, preferred_element_type=F32))


def _silu(y):
    h = 0.5 * y
    return h + h * jnp.tanh(h)


def _in_proj_kernel(tiles_per_seq, x_ref, nw_ref, w_ref, wg_ref, cw_ref, lnw_ref, lnb_ref,
                    out_ref, gate_ref, xn_ref, accp_ref, halo_ref, full_ref):
    i = pl.program_id(0)
    j = pl.program_id(1)

    @pl.when(j == 0)
    def _():
        x = x_ref[...]
        ms = jnp.mean(x * x, axis=-1, keepdims=True)
        xn = ((x * lax.rsqrt(ms + EPS)) * nw_ref[...]).astype(BF16)
        xn_ref[...] = xn
        gate_ref[...] = _mm_nt(xn, wg_ref[...])

    @pl.when((j == 0) & (i == 0))
    def _():
        halo_ref[...] = jnp.zeros_like(halo_ref)

    n_row_groups = PROJ_TM // SUB_M
    units = [(c, r) for c in range(N_SUB) for r in range(n_row_groups)]

    def unit_dot(c, r):
        return _mm_nt(xn_ref[r * SUB_M:(r + 1) * SUB_M, :], w_ref[c * SUB_W:(c + 1) * SUB_W, :])

    def pipelined(stage, finish):
        for n, unit in enumerate(units):
            stage(*unit)
            if n > 0:
                finish(*units[n - 1])
        finish(*units[-1])

    def rows_cols(c, r):
        return slice(r * SUB_M, (r + 1) * SUB_M), slice(c * SUB_W, (c + 1) * SUB_W)

    def conv_branch(l2_norm):
        seq_start = (i % tiles_per_seq) == 0
        post = jnp.where(j == COL_Q, HEAD_DIM ** -0.5, 1.0)

        def stage(c, r):
            acc = unit_dot(c, r)
            hidx = j * N_SUB + c
            if r == 0:
                accp_ref[c, 0:CONV_HALO, :] = jnp.where(seq_start, 0.0, halo_ref[hidx])
            accp_ref[c, CONV_HALO + r * SUB_M:CONV_HALO + (r + 1) * SUB_M, :] = acc
            if r == n_row_groups - 1:
                halo_ref[hidx] = acc[SUB_M - CONV_HALO:SUB_M, :]

        def finish(c, r):
            rows, cols = rows_cols(c, r)
            xp = accp_ref[c, r * SUB_M:r * SUB_M + CONV_HALO + SUB_M, :]
            x1 = pltpu.roll(xp, 1, axis=0)
            near = cw_ref[3:4, cols] * xp + cw_ref[2:3, cols] * x1
            far = cw_ref[1:2, cols] * xp + cw_ref[0:1, cols] * x1
            z = near + pltpu.roll(far, 2, axis=0)
            y = _silu(z[CONV_HALO:, :])
            if l2_norm:
                parts = []
                for hh in range(SUB_W // HEAD_DIM):
                    yh = y[:, hh * HEAD_DIM:(hh + 1) * HEAD_DIM]
                    ss = jnp.sum(yh * yh, axis=-1, keepdims=True)
                    parts.append(yh * (lax.rsqrt(ss + EPS) * post))
                y = jnp.concatenate(parts, axis=1)
            out_ref[rows, cols] = y.astype(BF16)

        pipelined(stage, finish)

    @pl.when(j < COL_V)
    def _():
        conv_branch(True)

    @pl.when(j == COL_V)
    def _():
        conv_branch(False)

    def stage_full(c, r):
        rows, cols = rows_cols(c, r)
        full_ref[rows, cols] = unit_dot(c, r)

    @pl.when((j == COL_ZA) | (j == COL_ZB))
    def _():
        def finish(c, r):
            rows, cols = rows_cols(c, r)
            out_ref[rows, cols] = _silu(full_ref[rows, cols]).astype(BF16)
        pipelined(stage_full, finish)

    @pl.when(j == COL_UB)
    def _():
        for (c, r) in units:
            rows, cols = rows_cols(c, r)
            out_ref[rows, cols] = unit_dot(c, r).astype(BF16)

    @pl.when(j == COL_VB)
    def _():
        def ln_rows(r):
            rows = slice(r * SUB_M, (r + 1) * SUB_M)
            s1 = jnp.zeros((SUB_M, 1), F32)
            for c in range(N_SUB):
                s1 = s1 + jnp.sum(full_ref[rows, c * SUB_W:(c + 1) * SUB_W], axis=-1, keepdims=True)
            mu = s1 * (1.0 / BLOCK_W)
            s2 = jnp.zeros((SUB_M, 1), F32)
            for c in range(N_SUB):
                d = full_ref[rows, c * SUB_W:(c + 1) * SUB_W] - mu
                s2 = s2 + jnp.sum(d * d, axis=-1, keepdims=True)
            rstd = lax.rsqrt(s2 * (1.0 / BLOCK_W) + EPS)
            for c in range(N_SUB):
                cols = slice(c * SUB_W, (c + 1) * SUB_W)
                d = full_ref[rows, cols] - mu
                out_ref[rows, cols] = ((d * rstd) * lnw_ref[:, cols] + lnb_ref[:, cols]).astype(BF16)

        for r in range(n_row_groups):
            stage_full(0, r)
            if r > 0:
                ln_rows(r - 1)
            for c in range(1, N_SUB):
                stage_full(c, r)
        ln_rows(n_row_groups - 1)


def _in_proj(x2, norm_w, w_t, w_gate_t, conv_w, ln_w, ln_b, seq):
    m, d = x2.shape

    def w_row_start(i, j):
        start = j * BLOCK_W + jnp.where(j > COL_ZA, 2 * A_HEADS, 0)
        return (pl.multiple_of(start, 2 * A_HEADS), 0)

    return pl.pallas_call(
        functools.partial(_in_proj_kernel, seq // PROJ_TM),
        grid=(m // PROJ_TM, MAIN_WIDTH // BLOCK_W),
        in_specs=[
            pl.BlockSpec((PROJ_TM, d), lambda i, j: (i, 0)),
            pl.BlockSpec((1, d), lambda i, j: (0, 0)),
            pl.BlockSpec((pl.Element(BLOCK_W), pl.Element(d)), w_row_start),
            pl.BlockSpec((2 * LANES, d), lambda i, j: (0, 0)),
            pl.BlockSpec((CONV_WIDTH, BLOCK_W), lambda i, j: (0, jnp.minimum(j, COL_V))),
            pl.BlockSpec((1, B_WIDTH), lambda i, j: (0, 0)),
            pl.BlockSpec((1, B_WIDTH), lambda i, j: (0, 0)),
        ],
        out_specs=[
            pl.BlockSpec((PROJ_TM, BLOCK_W), lambda i, j: (i, j)),
            pl.BlockSpec((PROJ_TM, 2 * LANES), lambda i, j: (i, 0)),
        ],
        out_shape=[
            jax.ShapeDtypeStruct((m, MAIN_WIDTH), BF16),
            jax.ShapeDtypeStruct((m, 2 * LANES), F32),
        ],
        scratch_shapes=[
            pltpu.VMEM((PROJ_TM, d), BF16),
            pltpu.VMEM((N_SUB, CONV_HALO + PROJ_TM, SUB_W), F32),
            pltpu.VMEM((3 * N_SUB, CONV_HALO, SUB_W), F32),
            pltpu.VMEM((PROJ_TM, BLOCK_W), F32),
        ],
        compiler_params=pltpu.CompilerParams(
            dimension_semantics=("arbitrary", "arbitrary"), vmem_limit_bytes=VMEM_LIMIT),
        name="in_proj",
    )(x2, norm_w, w_t, w_gate_t, conv_w, ln_w, ln_b)


def _gdn_kernel(q_ref, k_ref, v_ref, za_ref, gate_ref, alog_ref, dtb_ref, hnw_ref,
                oa_ref, s_ref):
    tt = GDN_TT
    c = CHUNK_A

    @pl.when(pl.program_id(1) == 0)
    def _():
        s_ref[...] = jnp.zeros_like(s_ref)

    ri = lax.broadcasted_iota(jnp.int32, (tt, tt), 0)
    ci = lax.broadcasted_iota(jnp.int32, (tt, tt), 1)
    same_chunk = (ri // c) == (ci // c)
    tri = jnp.where(same_chunk & (ci <= ri), 1.0, 0.0).astype(BF16)

    beta, g, e_gc, e_kd, e_gl = [], [], [], [], []
    for bi in range(GDN_NB):
        gates = gate_ref[bi]
        beta.append(1.0 / (1.0 + jnp.exp(-gates[:, :LANES])))
        a_in = gates[:, LANES:] + dtb_ref[...]
        softplus = jnp.maximum(a_in, 0.0) + jnp.log(1.0 + jnp.exp(-jnp.abs(a_in)))
        g.append(-jnp.exp(alog_ref[...]) * softplus)
        gc = _mm_split(tri, g[bi])
        gl = jnp.concatenate(
            [jnp.broadcast_to(gc[(n + 1) * c - 1:(n + 1) * c, :], (c, LANES)) for n in range(tt // c)],
            axis=0)
        e_gc.append(jnp.exp(gc))
        e_kd.append(jnp.exp(gl - gc))
        e_gl.append(jnp.exp(gl))

    n_chunks = tt // c
    pairs_per_row = A_HEADS // 2
    n_heads = GDN_NB * A_HEADS
    n_pairs = n_heads // 2
    n_quads = n_heads // 4
    row_q = lax.broadcasted_iota(jnp.int32, (c, 4 * c), 0)
    col_q = lax.broadcasted_iota(jnp.int32, (c, 4 * c), 1) % c
    incl = row_q >= col_q
    strict = row_q > col_q
    eye = jnp.where(row_q == col_q, 1.0, 0.0).astype(F32)
    low_half = lax.broadcasted_iota(jnp.int32, (c, LANES), 1) < c
    r64 = lax.broadcasted_iota(jnp.int32, (c, c), 0)
    c64 = lax.broadcasted_iota(jnp.int32, (c, c), 1)
    tri_c = jnp.where(r64 >= c64, 1.0, 0.0).astype(BF16)
    zeros_blk = jnp.zeros((c, LANES), BF16)
    hnw = hnw_ref[...]

    def rows_of(n):
        return slice(n * c, (n + 1) * c)

    def lanes_of(h, width=LANES):
        return slice(h * width, (h + 1) * width)

    def bcast(arrs, n, h):
        col = h % A_HEADS
        return jnp.broadcast_to(arrs[h // A_HEADS][rows_of(n), col:col + 1], (c, LANES))

    def pair_block(ref, n, p):
        return ref[p // pairs_per_row, rows_of(n), lanes_of(p % pairs_per_row, 2 * LANES)]

    def chunk_decay(n, h):
        col = h % A_HEADS
        return e_gl[h // A_HEADS][n * c:n * c + 1, col:col + 1]

    def pair_bcast(arr, n, p):
        return jnp.concatenate([bcast(arr, n, 2 * p), bcast(arr, n, 2 * p + 1)], axis=1)

    def block_diag_pair(x_pair):
        xb = x_pair.astype(BF16)
        zero = jnp.zeros((xb.shape[0], LANES), BF16)
        top = jnp.concatenate([xb[:, :LANES], zero], axis=1)
        bot = jnp.concatenate([zero, xb[:, LANES:]], axis=1)
        return jnp.concatenate([top, bot], axis=0)

    def block_diag_quad(m_quad):
        mb = m_quad.astype(BF16)
        left, right = mb[:, :LANES], mb[:, LANES:]
        zero = jnp.zeros_like(left)
        blocks = [
            jnp.concatenate([jnp.where(low_half, left, zero), zeros_blk], axis=1),
            jnp.concatenate([jnp.where(low_half, zero, left), zeros_blk], axis=1),
            jnp.concatenate([zeros_blk, jnp.where(low_half, right, zero)], axis=1),
            jnp.concatenate([zeros_blk, jnp.where(low_half, zero, right)], axis=1),
        ]
        return jnp.concatenate(blocks, axis=0)

    chunk_pairs = [(n, p) for n in range(n_chunks) for p in range(n_pairs)]
    chunk_quads = [(n, qd) for n in range(n_chunks) for qd in range(n_quads)]

    kb, kk = {}, {}
    for (n, p) in chunk_pairs:
        k_pair = pair_block(k_ref, n, p)
        kb[n, p] = k_pair.astype(F32) * pair_bcast(beta, n, p)
        lhs = jnp.concatenate([pair_block(q_ref, n, p), kb[n, p].astype(BF16)], axis=0)
        kk[n, p] = _mm_nt(lhs, block_diag_pair(k_pair))

    attn, p_inv, m_pow = {}, {}, {}
    for (n, qd) in chunk_quads:
        gb = [bcast(g, n, 4 * qd + hh) for hh in range(4)]
        g_quad = jnp.concatenate([jnp.where(low_half, gb[0], gb[1]),
                                  jnp.where(low_half, gb[2], gb[3])], axis=1)
        diff = _mm_split(tri_c, jnp.where(strict, g_quad, 0.0))
        decay = jnp.where(incl, jnp.exp(jnp.where(incl, diff, 0.0)), 0.0)
        qk = jnp.concatenate([kk[n, 2 * qd][:c], kk[n, 2 * qd + 1][:c]], axis=1)
        kbk = jnp.concatenate([kk[n, 2 * qd][c:], kk[n, 2 * qd + 1][c:]], axis=1)
        attn[n, qd] = qk * decay
        lmat = jnp.where(strict, kbk * decay, 0.0)
        p_inv[n, qd] = eye - lmat
        m_pow[n, qd] = lmat
    m_pow = {key: _mm(m_pow[key], block_diag_quad(m_pow[key])) for key in chunk_quads}
    for it in range(5):
        bd = {key: block_diag_quad(m_pow[key]) for key in chunk_quads}
        if it < 4:
            both = {key: _mm(jnp.concatenate([p_inv[key], m_pow[key]], axis=0), bd[key])
                    for key in chunk_quads}
            p_inv = {key: p_inv[key] + both[key][:c] for key in chunk_quads}
            m_pow = {key: both[key][c:] for key in chunk_quads}
        else:
            p_inv = {key: p_inv[key] + _mm(p_inv[key], bd[key]) for key in chunk_quads}

    u, w = {}, {}
    for (n, p) in chunk_pairs:
        vb = pair_block(v_ref, n, p).astype(F32) * pair_bcast(beta, n, p)
        kbe = kb[n, p] * pair_bcast(e_gc, n, p)
        rhs = jnp.concatenate([block_diag_pair(vb), block_diag_pair(kbe)], axis=1)
        sol = _mm(p_inv[n, p // 2][:, lanes_of(p % 2)], rhs)
        u[n, p] = sol[:, :2 * LANES]
        w[n, p] = sol[:, 2 * LANES:]

    kdec_t = {}
    for (n, p) in chunk_pairs:
        kdec = pair_block(k_ref, n, p).astype(F32) * pair_bcast(e_kd, n, p)
        kdec_t[n, p] = jnp.concatenate([kdec[:, :LANES], kdec[:, LANES:]], axis=0).T

    s = [s_ref[p] for p in range(n_pairs)]
    for n in range(n_chunks):
        qe = [pair_block(q_ref, n, p).astype(F32) * pair_bcast(e_gc, n, p) for p in range(n_pairs)]
        ws = [_mm(jnp.concatenate([w[n, p], qe[p]], axis=0), block_diag_pair(s[p]))
              for p in range(n_pairs)]
        v_new = [u[n, p] - ws[p][:c] for p in range(n_pairs)]
        mixed = [_mm(jnp.concatenate([attn[n, p // 2][:, lanes_of(p % 2)].astype(BF16),
                                      kdec_t[n, p].astype(BF16)], axis=0),
                     block_diag_pair(v_new[p])) for p in range(n_pairs)]
        o = [ws[p][c:] + mixed[p][:c] for p in range(n_pairs)]
        s = [jnp.concatenate([s[p][:, :LANES] * chunk_decay(n, 2 * p),
                              s[p][:, LANES:] * chunk_decay(n, 2 * p + 1)], axis=1)
             + mixed[p][c:] for p in range(n_pairs)]
        for h in range(n_heads):
            oh = o[h // 2][:, lanes_of(h % 2)]
            on = (oh * lax.rsqrt(jnp.mean(oh * oh, axis=-1, keepdims=True) + EPS)) * hnw
            bi, col = h // A_HEADS, h % A_HEADS
            sz = za_ref[bi, rows_of(n), lanes_of(col)].astype(F32)
            oa_ref[bi, rows_of(n), lanes_of(col)] = (on * sz).astype(BF16)
    for p in range(n_pairs):
        s_ref[p] = s[p]


def _gdn(act, gates, alog_lane, dtb_lane, hnw, batch, seq):
    def col_block(cb):
        return pl.BlockSpec((GDN_NB, GDN_TT, BLOCK_W), lambda b, t: (b, t, cb))

    return pl.pallas_call(
        _gdn_kernel,
        grid=(batch // GDN_NB, seq // GDN_TT),
        in_specs=[
            col_block(COL_Q), col_block(COL_K), col_block(COL_V), col_block(COL_ZA),
            pl.BlockSpec((GDN_NB, GDN_TT, 2 * LANES), lambda b, t: (b, t, 0)),
            pl.BlockSpec((1, LANES), lambda b, t: (0, 0)),
            pl.BlockSpec((1, LANES), lambda b, t: (0, 0)),
            pl.BlockSpec((1, HEAD_DIM), lambda b, t: (0, 0)),
        ],
        out_specs=pl.BlockSpec((GDN_NB, GDN_TT, A_WIDTH), lambda b, t: (b, t, 0)),
        out_shape=jax.ShapeDtypeStruct((batch, seq, A_WIDTH), BF16),
        scratch_shapes=[pltpu.VMEM((GDN_NB * A_HEADS // 2, HEAD_DIM, 2 * HEAD_DIM), F32)],
        compiler_params=pltpu.CompilerParams(
            dimension_semantics=("arbitrary", "arbitrary"), vmem_limit_bytes=VMEM_LIMIT),
        name="gdn",
    )(act, act, act, act, gates, alog_lane, dtb_lane, hnw)


def _out_proj_kernel(oa_ref, ub_ref, vn_ref, zb_ref, x_ref, ws_ref, bs_ref, wa_ref, wb_ref,
                     fnw_ref, out_ref, ob_ref):
    row = lax.broadcasted_iota(jnp.int32, (CHUNK_B, CHUNK_B), 0)
    col = lax.broadcasted_iota(jnp.int32, (CHUNK_B, CHUNK_B), 1)
    causal = row >= col
    bias = bs_ref[...]
    for gidx in range(B_GROUPS):
        cols = slice(gidx * GROUP_DIM, (gidx + 1) * GROUP_DIM)
        w_m = jnp.where(causal, ws_ref[gidx], 0.0).astype(BF16)
        b_col = bias[:, gidx:gidx + 1]
        for n in range(OUT_TM // CHUNK_B):
            rows = slice(n * CHUNK_B, (n + 1) * CHUNK_B)
            s = jnp.dot(w_m, vn_ref[rows, cols], preferred_element_type=F32) + b_col
            u = ub_ref[rows, cols].astype(F32)
            sz = zb_ref[rows, cols].astype(F32)
            ob_ref[rows, cols] = (u * s * sz).astype(BF16)

    mix = (jnp.dot(oa_ref[...], wa_ref[...], preferred_element_type=F32)
           + jnp.dot(ob_ref[...], wb_ref[...], preferred_element_type=F32))
    h = x_ref[...] + mix
    ms = jnp.mean(h * h, axis=-1, keepdims=True)
    out_ref[...] = (h * lax.rsqrt(ms + EPS)) * fnw_ref[...]


def _out_proj(o_a, act, x2, w_spatial, bias_t, w_out_bf, final_norm_w):
    m, d = x2.shape

    def col_block(cb):
        return pl.BlockSpec((OUT_TM, BLOCK_W), lambda i: (i, cb))

    return pl.pallas_call(
        _out_proj_kernel,
        grid=(m // OUT_TM,),
        in_specs=[
            pl.BlockSpec((OUT_TM, A_WIDTH), lambda i: (i, 0)),
            col_block(COL_UB), col_block(COL_VB), col_block(COL_ZB),
            pl.BlockSpec((OUT_TM, d), lambda i: (i, 0)),
            pl.BlockSpec((B_GROUPS, CHUNK_B, CHUNK_B), lambda i: (0, 0, 0)),
            pl.BlockSpec((CHUNK_B, LANES), lambda i: (0, 0)),
            pl.BlockSpec((A_WIDTH, d), lambda i: (0, 0)),
            pl.BlockSpec((B_WIDTH, d), lambda i: (1, 0)),
            pl.BlockSpec((1, d), lambda i: (0, 0)),
        ],
        out_specs=pl.BlockSpec((OUT_TM, d), lambda i: (i, 0)),
        out_shape=jax.ShapeDtypeStruct((m, d), F32),
        scratch_shapes=[pltpu.VMEM((OUT_TM, B_WIDTH), BF16)],
        compiler_params=pltpu.CompilerParams(
            dimension_semantics=("arbitrary",), vmem_limit_bytes=VMEM_LIMIT),
        name="out_proj",
    )(o_a, act, act, act, x2, w_spatial, bias_t, w_out_bf, w_out_bf, final_norm_w)


def _lane_row(vec):
    return jnp.pad(vec.astype(F32), (0, LANES - vec.shape[0])).reshape(1, LANES)


def kernel(x, norm_w, w_in, conv_w, a_log, dt_bias, head_norm_w, sgu_ln_w, sgu_ln_b,
           w_spatial, b_spatial, w_out, final_norm_w):
    batch, seq, d = x.shape
    assert norm_w.shape[0] == 1, "single-layer problem"
    assert seq % PROJ_TM == 0 and seq % GDN_TT == 0 and (batch * seq) % OUT_TM == 0
    assert batch % GDN_NB == 0
    x2 = x.reshape(batch * seq, d)

    w_t = jnp.swapaxes(w_in[0], 0, 1).astype(BF16)
    gate0 = (COL_ZA + 1) * BLOCK_W
    zero_rows = jnp.zeros((LANES - A_HEADS, d), BF16)
    w_gate_t = jnp.concatenate([w_t[gate0:gate0 + A_HEADS], zero_rows,
                                w_t[gate0 + A_HEADS:gate0 + 2 * A_HEADS], zero_rows], axis=0)

    act, gates = _in_proj(x2, norm_w, w_t, w_gate_t, conv_w[0], sgu_ln_w, sgu_ln_b, seq)
    o_a = _gdn(act.reshape(batch, seq, MAIN_WIDTH), gates.reshape(batch, seq, 2 * LANES),
               _lane_row(a_log[0]), _lane_row(dt_bias[0]), head_norm_w, batch, seq)
    o_a = o_a.reshape(batch * seq, A_WIDTH)
    bias_t = jnp.pad(b_spatial[0].T.astype(F32), ((0, 0), (0, LANES - B_GROUPS)))
    out = _out_proj(o_a, act, x2, w_spatial[0], bias_t, w_out[0].astype(BF16),
                    final_norm_w.reshape(1, d))
    return out.reshape(batch, seq, d)
```
